```python
import math
import jax, jax.numpy as jnp
from jax import lax
import numpy as np

D_MODEL = 1024
BATCH = 8
SEQ = 8192
DEPTH = 2
DEC_BATCH = 32
DEC_SEQ = 64
PAST_LEN = 2048

CHUNK = 64
N_MIXERS = 2
N_CONV_LAYERS = (DEPTH + 1) // 2
N_GDN_LAYERS = DEPTH // 2
CONV_WIDTH = 31
GDN_QK_HEADS = 8
GDN_V_HEADS = 16
GDN_HEAD_DIM = 128
GDN_QK_WIDTH = GDN_QK_HEADS * GDN_HEAD_DIM
GDN_V_WIDTH = GDN_V_HEADS * GDN_HEAD_DIM
GDN_CONV_CH = 2 * GDN_QK_WIDTH + GDN_V_WIDTH
GDN_IN_WIDTH = GDN_CONV_CH + GDN_V_WIDTH + 2 * GDN_V_HEADS
SHORT_CONV_WIDTH = 4
FFN_HIDDEN = ((8 * D_MODEL + 3 * 256 - 1) // (3 * 256)) * 256
EPS = 1e-6

kernel_name = "hybrid_conformer_gdn_stream_step"


def rms_norm(x, g):
    xf = x.astype(jnp.float32)
    y = xf * lax.rsqrt(jnp.mean(xf * xf, axis=-1, keepdims=True) + EPS)
    return (y * g.astype(jnp.float32)).astype(x.dtype)


def layer_norm(x, g, b):
    xf = x.astype(jnp.float32)
    mu = jnp.mean(xf, axis=-1, keepdims=True)
    var = jnp.mean(jnp.square(xf - mu), axis=-1, keepdims=True)
    y = (xf - mu) * lax.rsqrt(var + EPS)
    return (y * g.astype(jnp.float32) + b.astype(jnp.float32)).astype(x.dtype)


def l2_normalize(x):
    xf = x.astype(jnp.float32)
    return xf * lax.rsqrt(jnp.sum(xf * xf, axis=-1, keepdims=True) + EPS)


def causal_depthwise_conv(ext, w):
    return lax.conv_general_dilated(
        ext, w[:, None, :].astype(ext.dtype), window_strides=(1,), padding='VALID',
        dimension_numbers=('NWC', 'WIO', 'NWC'), feature_group_count=ext.shape[-1])


def modulation(c, w_ada, b_ada):
    mod = jax.nn.silu(c) @ w_ada + b_ada
    return jnp.split(mod[:, None, :], 6, axis=-1)


def conformer_conv(h, buf, w_pw1, b_pw1, w_dw, b_dw, ln_g, ln_b, w_pw2, b_pw2):
    a = h @ w_pw1 + b_pw1
    u = a[..., :D_MODEL] * jax.nn.sigmoid(a[..., D_MODEL:])
    ext = jnp.concatenate([buf.astype(u.dtype), u], axis=1)
    y = causal_depthwise_conv(ext, w_dw) + b_dw
    y = jax.nn.silu(layer_norm(y, ln_g, ln_b))
    return y @ w_pw2 + b_pw2, ext[:, -(CONV_WIDTH - 1):]


def gated_delta_rule(q, k, v, g, beta, S0, chunk):
    B, L, H, DK = q.shape
    DV = v.shape[-1]
    n = L // chunk

    def blk(t):
        t = t.astype(jnp.float32).reshape((B, n, chunk) + t.shape[2:])
        return jnp.moveaxis(t, 2, 3)

    q, k, v, g, beta = blk(q), blk(k), blk(v), blk(g), blk(beta)
    gc = jnp.cumsum(g, axis=-1)
    pos = jnp.arange(chunk)
    incl = pos[:, None] >= pos[None, :]
    strict = pos[:, None] > pos[None, :]
    decay = jnp.exp(jnp.where(incl, gc[..., :, None] - gc[..., None, :], -jnp.inf))
    kb = k * beta[..., None]
    m = jnp.where(strict, jnp.einsum('bnhcd,bnhsd->bnhcs', kb, k) * decay, 0.0)
    t_mat = m + jnp.eye(chunk, dtype=jnp.float32)
    rhs = jnp.concatenate([v * beta[..., None], kb * jnp.exp(gc)[..., None]], axis=-1)
    sol = lax.linalg.triangular_solve(t_mat, rhs, left_side=True, lower=True, unit_diagonal=True)
    u, w = sol[..., :DV], sol[..., DV:]
    qk = jnp.einsum('bnhcd,bnhsd->bnhcs', q, k) * decay
    q_dec = q * jnp.exp(gc)[..., None]
    k_dec = k * jnp.exp(gc[..., -1:] - gc)[..., None]
    g_tot = jnp.exp(gc[..., -1])

    def step(S, xs):
        u_c, w_c, qk_c, qd_c, kd_c, gt_c = xs
        v_new = u_c - jnp.einsum('bhcd,bhdv->bhcv', w_c, S)
        o = jnp.einsum('bhcd,bhdv->bhcv', qd_c, S) + jnp.einsum('bhcs,bhsv->bhcv', qk_c, v_new)
        S = S * gt_c[..., None, None] + jnp.einsum('bhcd,bhcv->bhdv', kd_c, v_new)
        return S, o

    xs = tuple(jnp.moveaxis(t, 1, 0) for t in (u, w, qk, q_dec, k_dec, g_tot))
    S, o = lax.scan(step, S0.astype(jnp.float32), xs)
    o = jnp.moveaxis(jnp.moveaxis(o, 0, 1), 3, 2).reshape(B, L, H, DV)
    return o, S


def gdn_mixer(h, conv_buf, S0, w_in, w_conv, A_log, dt_bias, g_norm, w_out, chunk):
    B, L, _ = h.shape
    p = h @ w_in
    qkv, z, a, b = jnp.split(p, [GDN_CONV_CH, GDN_CONV_CH + GDN_V_WIDTH,
                                 GDN_CONV_CH + GDN_V_WIDTH + GDN_V_HEADS], axis=-1)
    ext = jnp.concatenate([conv_buf.astype(qkv.dtype), qkv], axis=1)
    new_buf = ext[:, -(SHORT_CONV_WIDTH - 1):]
    qkv = jax.nn.silu(causal_depthwise_conv(ext, w_conv))
    q, k, v = jnp.split(qkv, [GDN_QK_WIDTH, 2 * GDN_QK_WIDTH], axis=-1)
    rep = GDN_V_HEADS // GDN_QK_HEADS
    q = jnp.repeat(l2_normalize(q.reshape(B, L, GDN_QK_HEADS, GDN_HEAD_DIM)), rep, axis=2) * (GDN_HEAD_DIM ** -0.5)
    k = jnp.repeat(l2_normalize(k.reshape(B, L, GDN_QK_HEADS, GDN_HEAD_DIM)), rep, axis=2)
    v = v.reshape(B, L, GDN_V_HEADS, GDN_HEAD_DIM)
    beta = jax.nn.sigmoid(b.astype(jnp.float32))
    g = -jnp.exp(A_log.astype(jnp.float32)) * jax.nn.softplus(a.astype(jnp.float32) + dt_bias.astype(jnp.float32))
    o, S = gated_delta_rule(q, k, v, g, beta, S0, chunk)
    o = rms_norm(o, g_norm) * jax.nn.silu(z.astype(jnp.float32).reshape(B, L, GDN_V_HEADS, GDN_HEAD_DIM))
    out = o.reshape(B, L, GDN_V_WIDTH).astype(h.dtype) @ w_out
    return out, new_buf, S.astype(S0.dtype)


def trunk(x, c, conv_bufs, gdn_states, gdn_bufs, p, chunk):
    new_conv, new_S, new_gbuf = [], [], []
    for i in range(DEPTH):
        sh1, sc1, gt1, sh2, sc2, gt2 = modulation(c, p['w_ada'][i], p['b_ada'][i])
        h = rms_norm(x, p['g_pre_mix'][i]) * (1 + sc1) + sh1
        j = i // N_MIXERS
        if i % N_MIXERS == 0:
            m, buf = conformer_conv(h, conv_bufs[j], p['w_pw1'][j], p['b_pw1'][j], p['w_dw'][j], p['b_dw'][j],
                                    p['ln_conv_g'][j], p['ln_conv_b'][j], p['w_pw2'][j], p['b_pw2'][j])
            new_conv.append(buf)
        else:
            m, buf, S = gdn_mixer(h, gdn_bufs[j], gdn_states[j], p['w_gdn_in'][j], p['w_gdn_conv'][j],
                                  p['gdn_A_log'][j], p['gdn_dt_bias'][j], p['g_gdn_out_norm'][j],
                                  p['w_gdn_out'][j], chunk)
            new_gbuf.append(buf)
            new_S.append(S)
        x = x + gt1 * rms_norm(m, p['g_post_mix'][i])
        h = rms_norm(x, p['g_pre_ffn'][i]) * (1 + sc2) + sh2
        f = (jax.nn.silu(h @ p['w_ffn_gate'][i]) * (h @ p['w_ffn_up'][i])) @ p['w_ffn_down'][i]
        x = x + gt2 * rms_norm(f, p['g_post_ffn'][i])
    return x, jnp.stack(new_conv), jnp.stack(new_S), jnp.stack(new_gbuf)


def setup_inputs(seed: int = 0) -> dict:
    key = jax.random.key(seed)
    ks = iter(jax.random.split(key, 48))
    D, NA, NB, F = D_MODEL, N_CONV_LAYERS, N_GDN_LAYERS, FFN_HIDDEN

    def nrm(shape, scale):
        return jax.random.normal(next(ks), shape, jnp.float32) * scale

    def gain(shape):
        return 1.0 + nrm(shape, 0.1)

    dt = jnp.exp(jax.random.uniform(next(ks), (NB, GDN_V_HEADS), jnp.float32,
                                    minval=math.log(1e-3), maxval=math.log(1e-1)))
    return {
        'x_prompt': nrm((BATCH, SEQ, D), 1.0),
        'x_sample': nrm((DEC_BATCH, DEC_SEQ, D), 1.0),
        'c_prompt': nrm((BATCH, D), 1.0),
        'c_sample': nrm((DEC_BATCH, D), 1.0),
        'cache_conv': nrm((NA, DEC_BATCH, CONV_WIDTH - 1, D), 0.5),
        'state_gdn': nrm((NB, DEC_BATCH, GDN_V_HEADS, GDN_HEAD_DIM, GDN_HEAD_DIM), 0.05),
        'cache_gdn_conv': nrm((NB, DEC_BATCH, SHORT_CONV_WIDTH - 1, GDN_CONV_CH), 1.0),
        'w_ada': nrm((DEPTH, D, 6 * D), 0.5 * D ** -0.5),
        'b_ada': nrm((DEPTH, 6 * D), 0.02),
        'g_pre_mix': gain((DEPTH, D)),
        'g_post_mix': gain((DEPTH, D)),
        'g_pre_ffn': gain((DEPTH, D)),
        'g_post_ffn': gain((DEPTH, D)),
        'w_ffn_gate': nrm((DEPTH, D, F), D ** -0.5),
        'w_ffn_up': nrm((DEPTH, D, F), D ** -0.5),
        'w_ffn_down': nrm((DEPTH, F, D), F ** -0.5),
        'w_pw1': nrm((NA, D, 2 * D), D ** -0.5),
        'b_pw1': nrm((NA, 2 * D), 0.02),
        'w_dw': nrm((NA, CONV_WIDTH, D), CONV_WIDTH ** -0.5),
        'b_dw': nrm((NA, D), 0.02),
        'ln_conv_g': gain((NA, D)),
        'ln_conv_b': nrm((NA, D), 0.02),
        'w_pw2': nrm((NA, D, D), D ** -0.5),
        'b_pw2': nrm((NA, D), 0.02),
        'w_gdn_in': nrm((NB, D, GDN_IN_WIDTH), D ** -0.5),
        'w_gdn_conv': nrm((NB, SHORT_CONV_WIDTH, GDN_CONV_CH), SHORT_CONV_WIDTH ** -0.5),
        'gdn_A_log': jnp.log(jax.random.uniform(next(ks), (NB, GDN_V_HEADS), jnp.float32, minval=1.0, maxval=16.0)),
        'gdn_dt_bias': dt + jnp.log(-jnp.expm1(-dt)),
        'g_gdn_out_norm': gain((NB, GDN_HEAD_DIM)),
        'w_gdn_out': nrm((NB, GDN_V_WIDTH, D), GDN_V_WIDTH ** -0.5),
    }


def reference(x_prompt, x_sample, c_prompt, c_sample, cache_conv, state_gdn, cache_gdn_conv,
              w_ada, b_ada, g_pre_mix, g_post_mix, g_pre_ffn, g_post_ffn,
              w_ffn_gate, w_ffn_up, w_ffn_down,
              w_pw1, b_pw1, w_dw, b_dw, ln_conv_g, ln_conv_b, w_pw2, b_pw2,
              w_gdn_in, w_gdn_conv, gdn_A_log, gdn_dt_bias, g_gdn_out_norm, w_gdn_out):
    p = dict(w_ada=w_ada, b_ada=b_ada, g_pre_mix=g_pre_mix, g_post_mix=g_post_mix,
             g_pre_ffn=g_pre_ffn, g_post_ffn=g_post_ffn,
             w_ffn_gate=w_ffn_gate, w_ffn_up=w_ffn_up, w_ffn_down=w_ffn_down,
             w_pw1=w_pw1, b_pw1=b_pw1, w_dw=w_dw, b_dw=b_dw, ln_conv_g=ln_conv_g, ln_conv_b=ln_conv_b,
             w_pw2=w_pw2, b_pw2=b_pw2, w_gdn_in=w_gdn_in, w_gdn_conv=w_gdn_conv, gdn_A_log=gdn_A_log,
             gdn_dt_bias=gdn_dt_bias, g_gdn_out_norm=g_gdn_out_norm, w_gdn_out=w_gdn_out)
    dt = x_prompt.dtype
    conv0 = jnp.zeros((N_CONV_LAYERS, BATCH, CONV_WIDTH - 1, D_MODEL), dt)
    S0 = jnp.zeros((N_GDN_LAYERS, BATCH, GDN_V_HEADS, GDN_HEAD_DIM, GDN_HEAD_DIM), state_gdn.dtype)
    gconv0 = jnp.zeros((N_GDN_LAYERS, BATCH, SHORT_CONV_WIDTH - 1, GDN_CONV_CH), dt)
    y_prompt, conv_p, S_p, gconv_p = trunk(x_prompt, c_prompt, conv0, S0, gconv0, p, CHUNK)
    y_sample, conv_s, S_s, gconv_s = trunk(x_sample, c_sample, cache_conv, state_gdn, cache_gdn_conv,
                                           p, x_sample.shape[1])
    return (y_prompt, y_sample, conv_p, conv_s, S_p, S_s, gconv_p, gconv_s)
```

```python
import functools

import jax
import jax.numpy as jnp
from jax import lax
from jax.experimental import pallas as pl
from jax.experimental.pallas import tpu as pltpu

F32 = jnp.float32
BF16 = jnp.bfloat16
EPS = 1e-6

SUBLANES = 8
LANES = 128
V7X_VMEM_BYTES = 64 * 1024 * 1024
VMEM_LIMIT = V7X_VMEM_BYTES * 7 // 8

CONV_WIDTH = 31
CONV_HIST = CONV_WIDTH - 1
CONV_PAD = 32
SHORT_WIDTH = 4
SHORT_HIST = SHORT_WIDTH - 1
SHORT_PAD = SUBLANES
HEAD_DIM = 128
QK_HEADS = 8
V_HEADS = 16
CHUNK = 64
CONV_ROWS = 16
CONV_COLS = 512


def _rms(x, g):
    return x * lax.rsqrt(jnp.mean(x * x, axis=-1, keepdims=True) + EPS) * g


def _silu(x):
    return x * jax.nn.sigmoid(x)


def _softplus(x):
    return jnp.maximum(x, 0.0) + jnp.log1p(jnp.exp(-jnp.abs(x)))


def _mm(a, b):
    return jnp.dot(a, b, preferred_element_type=F32)


def _mm_nt(a, b):
    return lax.dot_general(a, b, (((1,), (1,)), ((), ())), preferred_element_type=F32)


def _const_spec(shape):
    zeros = (0,) * len(shape)
    return pl.BlockSpec(shape, lambda *_: zeros, pipeline_mode=pl.Buffered(1))


def _params(n_axes):
    return pltpu.CompilerParams(dimension_semantics=("arbitrary",) * n_axes, vmem_limit_bytes=VMEM_LIMIT)


def _modulation_kernel(c_ref, w_ref, b_ref, o_ref):
    c = c_ref[...]
    o_ref[...] = _mm(_silu(c).astype(BF16), w_ref[...].astype(BF16)) + b_ref[...]


def _modulation(c, w_ada, b_ada):
    depth, d, n = w_ada.shape
    rows = c.shape[0]
    bn = n // 4
    return pl.pallas_call(
        _modulation_kernel,
        grid=(depth, n // bn),
        in_specs=[
            pl.BlockSpec((rows, d), lambda i, j: (0, 0)),
            pl.BlockSpec((None, d, bn), lambda i, j: (i, 0, j)),
            pl.BlockSpec((None, 1, bn), lambda i, j: (i, 0, j)),
        ],
        out_specs=pl.BlockSpec((None, rows, bn), lambda i, j: (i, 0, j)),
        out_shape=jax.ShapeDtypeStruct((depth, rows, n), F32),
        compiler_params=_params(2),
        name="modulation",
    )(c, w_ada, b_ada.reshape(depth, 1, n))


def _conv_mixer_kernel(x_ref, mod_ref, hist_ref, gpre_ref, wpw1_ref, bpw1_ref, wdw_ref, bdw_ref,
                       lng_ref, lnb_ref, wpw2_ref, bpw2_ref, gpost_ref,
                       out_ref, cache_ref, ext_ref, sh_ref, y_ref):
    bt, tt, d = x_ref.shape

    @pl.when(pl.program_id(1) == 0)
    def _():
        ext_ref[:, CONV_PAD - CONV_HIST:CONV_PAD, :] = hist_ref[...]

    x = x_ref[...]
    mod = mod_ref[...]
    shift, scale, gate = mod[:, :, :d], mod[:, :, d:2 * d], mod[:, :, 2 * d:]
    h = _rms(x, gpre_ref[...]) * (1.0 + scale) + shift
    a = _mm(h.reshape(bt * tt, d).astype(BF16), wpw1_ref[...]) + bpw1_ref[...]
    u = a[:, :d] * jax.nn.sigmoid(a[:, d:])
    ext_ref[:, CONV_PAD:CONV_PAD + tt, :] = u.reshape(bt, tt, d)

    for r in range(1, SUBLANES):
        sh_ref[r - 1] = ext_ref[:, r:r + tt + CONV_PAD - SUBLANES, :]

    def row_block(i, carry):
        r0 = pl.multiple_of(i * CONV_ROWS, CONV_ROWS)
        for b in range(bt):
            for c0 in range(0, d, CONV_COLS):
                cols = slice(c0, c0 + CONV_COLS)
                acc = jnp.broadcast_to(bdw_ref[:, cols], (CONV_ROWS, CONV_COLS))
                for k in range(CONV_WIDTH):
                    q, r = divmod(CONV_PAD - CONV_HIST + k, SUBLANES)
                    rows = pl.ds(r0 + q * SUBLANES, CONV_ROWS)
                    tap = ext_ref[b, rows, cols] if r == 0 else sh_ref[r - 1, b, rows, cols]
                    acc = acc + tap * wdw_ref[k:k + 1, cols]
                y_ref[b, pl.ds(r0, CONV_ROWS), cols] = acc
        return carry

    lax.fori_loop(0, tt // CONV_ROWS, row_block, 0)

    y = y_ref[...].reshape(bt * tt, d)
    mu = jnp.mean(y, axis=-1, keepdims=True)
    yc = y - mu
    var = jnp.mean(yc * yc, axis=-1, keepdims=True)
    yn = yc * lax.rsqrt(var + EPS) * lng_ref[...] + lnb_ref[...]
    m = _mm(_silu(yn).astype(BF16), wpw2_ref[...]) + bpw2_ref[...]
    out_ref[...] = x + gate * _rms(m, gpost_ref[...]).reshape(bt, tt, d)

    tail = ext_ref[:, CONV_PAD + tt - CONV_HIST:CONV_PAD + tt, :]
    cache_ref[...] = tail
    ext_ref[:, CONV_PAD - CONV_HIST:CONV_PAD, :] = tail


def _conv_mixer(x, mod, hist, p, *, bt, tt):
    bsz, seq, d = x.shape
    assert bsz % bt == 0 and seq % tt == 0 and tt % CONV_ROWS == 0 and tt >= CONV_HIST
    tok = pl.BlockSpec((bt, tt, d), lambda b, t: (b, t, 0))
    per_batch = lambda rows, cols: pl.BlockSpec((bt, rows, cols), lambda b, t: (b, 0, 0))
    return pl.pallas_call(
        _conv_mixer_kernel,
        grid=(bsz // bt, seq // tt),
        in_specs=[tok, per_batch(1, 3 * d), per_batch(CONV_HIST, d),
                  _const_spec((1, d)), _const_spec((d, 2 * d)), _const_spec((1, 2 * d)),
                  _const_spec((CONV_WIDTH, d)), _const_spec((1, d)), _const_spec((1, d)), _const_spec((1, d)),
                  _const_spec((d, d)), _const_spec((1, d)), _const_spec((1, d))],
        out_specs=[tok, per_batch(CONV_HIST, d)],
        out_shape=[jax.ShapeDtypeStruct(x.shape, F32), jax.ShapeDtypeStruct((bsz, CONV_HIST, d), F32)],
        scratch_shapes=[pltpu.VMEM((bt, CONV_PAD + tt, d), F32),
                        pltpu.VMEM((SUBLANES - 1, bt, CONV_PAD + tt - SUBLANES, d), F32),
                        pltpu.VMEM((bt, tt, d), F32)],
        compiler_params=_params(2),
        name="conv_mixer",
    )(x, mod, hist, p["g_pre"], p["w_pw1"], p["b_pw1"], p["w_dw"], p["b_dw"], p["ln_g"], p["ln_b"],
      p["w_pw2"], p["b_pw2"], p["g_post"])


def _ffn_kernel(x_ref, mod_ref, gpre_ref, wg_ref, wu_ref, wd_ref, gpost_ref, out_ref):
    bt, tt, d = x_ref.shape
    x = x_ref[...]
    mod = mod_ref[...]
    shift, scale, gate = mod[:, :, :d], mod[:, :, d:2 * d], mod[:, :, 2 * d:]
    h = (_rms(x, gpre_ref[...]) * (1.0 + scale) + shift).reshape(bt * tt, d).astype(BF16)
    hid = _silu(_mm(h, wg_ref[...])) * _mm(h, wu_ref[...])
    f = _mm(hid.astype(BF16), wd_ref[...])
    out_ref[...] = x + gate * _rms(f, gpost_ref[...]).reshape(bt, tt, d)


def _ffn(x, mod, p, *, bt, tt):
    bsz, seq, d = x.shape
    f = p["w_gate"].shape[1]
    assert bsz % bt == 0 and seq % tt == 0
    tok = pl.BlockSpec((bt, tt, d), lambda b, t: (b, t, 0))
    return pl.pallas_call(
        _ffn_kernel,
        grid=(bsz // bt, seq // tt),
        in_specs=[tok, pl.BlockSpec((bt, 1, 3 * d), lambda b, t: (b, 0, 1)),
                  _const_spec((1, d)), _const_spec((d, f)), _const_spec((d, f)), _const_spec((f, d)),
                  _const_spec((1, d))],
        out_specs=tok,
        out_shape=jax.ShapeDtypeStruct(x.shape, F32),
        compiler_params=_params(2),
        name="ffn",
    )(x, mod, p["g_pre"], p["w_gate"], p["w_up"], p["w_down"], p["g_post"])


def _unit_lower_inverse(m):
    n = m.shape[0]
    r = lax.broadcasted_iota(jnp.int32, (n, n), 0)
    c = lax.broadcasted_iota(jnp.int32, (n, n), 1)
    x = (r == c).astype(F32) - jnp.where((r >> 1) == (c >> 1), m, 0.0)
    level = 1
    while (1 << level) < n:
        couple = ((r >> (level + 1)) == (c >> (level + 1))) & ((r >> level) != (c >> level))
        xb = x.astype(BF16)
        x = x - _mm(_mm(xb, jnp.where(couple, m, 0.0).astype(BF16)).astype(BF16), xb)
        level += 1
    return x


def _gdn_kernel(x_ref, mod_ref, hist_ref, s0_ref, gpre_ref, wqkv_ref, wz_ref, wab_ref, wconv_ref,
                alog_ref, dtb_ref, gnorm_ref, wout_ref, gpost_ref,
                out_ref, cache_ref, s_ref,
                ext_ref, hb_ref, q_ref, k_ref, v_ref, o_ref, gcb_ref, bb_ref, gr_ref):
    bt, tt, d = x_ref.shape
    rows = bt * tt
    n_chunks = tt // CHUNK
    qk_width = QK_HEADS * HEAD_DIM

    @pl.when(pl.program_id(1) == 0)
    def _():
        ext_ref[:, SHORT_PAD - SHORT_HIST:SHORT_PAD, :] = hist_ref[...]
        s_ref[...] = s0_ref[...]

    x = x_ref[...]
    mod = mod_ref[...]
    shift, scale, gate = mod[:, :, :d], mod[:, :, d:2 * d], mod[:, :, 2 * d:]
    hb = (_rms(x, gpre_ref[...]) * (1.0 + scale) + shift).reshape(rows, d).astype(BF16)
    hb_ref[...] = hb

    ext_ref[:, SHORT_PAD:SHORT_PAD + tt, :] = _mm(hb, wqkv_ref[...]).reshape(bt, tt, -1)
    conv = ext_ref[:, SHORT_PAD - SHORT_HIST:SHORT_PAD - SHORT_HIST + tt, :] * wconv_ref[0:1, :]
    for k in range(1, SHORT_WIDTH):
        lo = SHORT_PAD - SHORT_HIST + k
        conv = conv + ext_ref[:, lo:lo + tt, :] * wconv_ref[k:k + 1, :]
    qkv = _silu(conv)
    tail = ext_ref[:, SHORT_PAD + tt - SHORT_HIST:SHORT_PAD + tt, :]
    cache_ref[...] = tail
    ext_ref[:, SHORT_PAD - SHORT_HIST:SHORT_PAD, :] = tail

    for b in range(bt):
        for j in range(QK_HEADS):
            qh = qkv[b, :, j * HEAD_DIM:(j + 1) * HEAD_DIM]
            kh = qkv[b, :, qk_width + j * HEAD_DIM:qk_width + (j + 1) * HEAD_DIM]
            q_ref[b * QK_HEADS + j] = qh * (lax.rsqrt(jnp.sum(qh * qh, axis=-1, keepdims=True) + EPS)
                                            * (HEAD_DIM ** -0.5))
            k_ref[b * QK_HEADS + j] = kh * lax.rsqrt(jnp.sum(kh * kh, axis=-1, keepdims=True) + EPS)
        for hh in range(V_HEADS):
            lo = 2 * qk_width + hh * HEAD_DIM
            v_ref[b * V_HEADS + hh] = qkv[b, :, lo:lo + HEAD_DIM]

    ab = _mm(hb, wab_ref[...])
    g = -jnp.exp(alog_ref[...]) * _softplus(ab[:, :LANES] + dtb_ref[...])
    beta = jax.nn.sigmoid(ab[:, LANES:])
    r_i = lax.broadcasted_iota(jnp.int32, (rows, rows), 0)
    c_i = lax.broadcasted_iota(jnp.int32, (rows, rows), 1)
    tri = ((r_i >= c_i) & (r_i // CHUNK == c_i // CHUNK)).astype(F32)
    gc = jnp.dot(tri, g, precision=lax.Precision.HIGHEST, preferred_element_type=F32)
    pad = (-rows) % LANES
    gc_t = (jnp.concatenate([gc, jnp.zeros((pad, LANES), F32)], axis=0) if pad else gc).T
    for b in range(bt):
        for hh in range(V_HEADS):
            n = b * V_HEADS + hh
            gcb_ref[n] = jnp.broadcast_to(gc[b * tt:(b + 1) * tt, hh:hh + 1], (tt, LANES))
            bb_ref[n] = jnp.broadcast_to(beta[b * tt:(b + 1) * tt, hh:hh + 1], (tt, LANES))
            gr_ref[n] = gc_t[hh:hh + 1, b * tt:(b + 1) * tt]

    r_c = lax.broadcasted_iota(jnp.int32, (CHUNK, CHUNK), 0)
    c_c = lax.broadcasted_iota(jnp.int32, (CHUNK, CHUNK), 1)
    incl = r_c >= c_c
    strict = r_c > c_c
    zeros_half = jnp.zeros((CHUNK, HEAD_DIM), F32)

    def head_pair(n, carry):
        for c in range(n_chunks):
            rs = slice(c * CHUNK, (c + 1) * CHUNK)
            qh = q_ref[n, rs, :]
            kh = k_ref[n, rs, :]
            kbf = kh.astype(BF16)
            prods = _mm_nt(jnp.concatenate([qh.astype(BF16), kbf], axis=0), kbf)
            a_qk, a_kk = prods[:CHUNK], prods[CHUNK:]
            for hh in range(2):
                vh = 2 * n + hh
                gcb = gcb_ref[vh, rs, :]
                bb = bb_ref[vh, rs, :]
                g_row = gr_ref[vh][:, rs]
                diff = gcb[:, :CHUNK] - g_row
                decay = jnp.where(incl, jnp.exp(jnp.where(incl, diff, 0.0)), 0.0)
                t_inv = _unit_lower_inverse(jnp.where(strict, a_kk * bb[:, :CHUNK] * decay, 0.0))
                eg = jnp.exp(gcb)
                rhs = jnp.concatenate([v_ref[vh, rs, :] * bb, kh * (bb * eg)], axis=1).astype(BF16)
                sol = _mm(t_inv.astype(BF16), rhs)
                u, w = sol[:, :HEAD_DIM], sol[:, HEAD_DIM:]
                s = s_ref[vh]
                wq = jnp.concatenate([w, qh * eg], axis=0).astype(BF16)
                ws_qs = _mm(wq, s.astype(BF16))
                v_new = u - ws_qs[:CHUNK]
                v_new_b = v_new.astype(BF16)
                o_ref[vh, rs, :] = ws_qs[CHUNK:] + _mm((a_qk * decay).astype(BF16), v_new_b)
                g_last = gcb[CHUNK - 1:CHUNK, :]
                k_dec = kh * jnp.exp(g_last - gcb)
                k_dec_t = jnp.concatenate([k_dec, zeros_half], axis=0).T.astype(BF16)
                v_pad = jnp.concatenate([v_new, zeros_half], axis=0).astype(BF16)
                s_ref[vh] = s * jnp.exp(g_last) + _mm(k_dec_t, v_pad)
        return carry

    lax.fori_loop(0, bt * QK_HEADS, head_pair, 0)

    z = _mm(hb_ref[...], wz_ref[...])
    gated = []
    for hh in range(V_HEADS):
        o_h = jnp.concatenate([o_ref[b * V_HEADS + hh] for b in range(bt)], axis=0)
        gated.append((_rms(o_h, gnorm_ref[...]) * _silu(z[:, hh * HEAD_DIM:(hh + 1) * HEAD_DIM])).astype(BF16))
    m = _mm(jnp.concatenate(gated, axis=1), wout_ref[...])
    out_ref[...] = x + gate * _rms(m, gpost_ref[...]).reshape(bt, tt, d)


def _gdn_mixer(x, mod, hist, s0, p, *, bt, tt):
    bsz, seq, d = x.shape
    ch = hist.shape[-1]
    v_width = V_HEADS * HEAD_DIM
    assert bsz % bt == 0 and seq % tt == 0 and tt % CHUNK == 0
    tok = pl.BlockSpec((bt, tt, d), lambda b, t: (b, t, 0))
    per_batch = lambda rows, cols: pl.BlockSpec((bt, rows, cols), lambda b, t: (b, 0, 0))
    state = pl.BlockSpec((bt * V_HEADS, HEAD_DIM, HEAD_DIM), lambda b, t: (b, 0, 0))
    heads = lambda n: pltpu.VMEM((bt * n, tt, HEAD_DIM), F32)
    return pl.pallas_call(
        _gdn_kernel,
        grid=(bsz // bt, seq // tt),
        in_specs=[tok, per_batch(1, 3 * d), per_batch(SHORT_HIST, ch), state,
                  _const_spec((1, d)), _const_spec((d, ch)), _const_spec((d, v_width)), _const_spec((d, 2 * LANES)),
                  _const_spec((SHORT_WIDTH, ch)), _const_spec((1, LANES)), _const_spec((1, LANES)),
                  _const_spec((1, HEAD_DIM)), _const_spec((v_width, d)), _const_spec((1, d))],
        out_specs=[tok, per_batch(SHORT_HIST, ch), state],
        out_shape=[jax.ShapeDtypeStruct(x.shape, F32), jax.ShapeDtypeStruct((bsz, SHORT_HIST, ch), F32),
                   jax.ShapeDtypeStruct(s0.shape, F32)],
        scratch_shapes=[pltpu.VMEM((bt, SHORT_PAD + tt, ch), F32), pltpu.VMEM((bt * tt, d), BF16),
                        heads(QK_HEADS), heads(QK_HEADS), heads(V_HEADS), heads(V_HEADS),
                        heads(V_HEADS), heads(V_HEADS), pltpu.VMEM((bt * V_HEADS, 1, tt), F32)],
        compiler_params=_params(2),
        name="gdn_mixer",
    )(x, mod, hist, s0, p["g_pre"], p["w_qkv"], p["w_z"], p["w_ab"], p["w_conv"], p["a_log"], p["dt_bias"],
      p["g_norm"], p["w_out"], p["g_post"])


def _row(v):
    return v.reshape(1, -1)


def _pad_lanes(v):
    return jnp.pad(v.reshape(1, -1), ((0, 0), (0, LANES - v.shape[-1])))


def _trunk(x, mod, conv_hist, s0, gdn_hist, w, *, conv_tile, ffn_tile, gdn_tile):
    bsz = x.shape[0]
    x, conv_cache = _conv_mixer(x, mod[0], conv_hist, w["conv"], bt=conv_tile[0], tt=conv_tile[1])
    x = _ffn(x, mod[0], w["ffn"][0], bt=ffn_tile[0], tt=ffn_tile[1])
    x, gdn_cache, s = _gdn_mixer(x, mod[1], gdn_hist, s0.reshape(bsz * V_HEADS, HEAD_DIM, HEAD_DIM), w["gdn"],
                                 bt=gdn_tile[0], tt=gdn_tile[1])
    x = _ffn(x, mod[1], w["ffn"][1], bt=ffn_tile[0], tt=ffn_tile[1])
    return x, conv_cache[None], s.reshape((1,) + s0.shape), gdn_cache[None]


def kernel(x_prompt, x_sample, c_prompt, c_sample, cache_conv, state_gdn, cache_gdn_conv, w_ada, b_ada, g_pre_mix, g_post_mix, g_pre_ffn, g_post_ffn, w_ffn_gate, w_ffn_up, w_ffn_down, w_pw1, b_pw1, w_dw, b_dw, ln_conv_g, ln_conv_b, w_pw2, b_pw2, w_gdn_in, w_gdn_conv, gdn_A_log, gdn_dt_bias, g_gdn_out_norm, w_gdn_out):
    n_prompt = x_prompt.shape[0]
    d = x_prompt.shape[-1]
    qkv_width = 2 * QK_HEADS * HEAD_DIM + V_HEADS * HEAD_DIM
    v_width = V_HEADS * HEAD_DIM

    mod = _modulation(jnp.concatenate([c_prompt, c_sample], axis=0), w_ada, b_ada)[:, :, None, :]
    mod_prompt, mod_sample = mod[:, :n_prompt], mod[:, n_prompt:]

    w_in = w_gdn_in[0]
    w_a = w_in[:, qkv_width + v_width:qkv_width + v_width + V_HEADS]
    w_b = w_in[:, qkv_width + v_width + V_HEADS:]
    lane_pad = ((0, 0), (0, LANES - V_HEADS))
    w = {
        "conv": dict(g_pre=_row(g_pre_mix[0]), w_pw1=w_pw1[0].astype(BF16), b_pw1=_row(b_pw1[0]), w_dw=w_dw[0],
                     b_dw=_row(b_dw[0]), ln_g=_row(ln_conv_g[0]), ln_b=_row(ln_conv_b[0]),
                     w_pw2=w_pw2[0].astype(BF16), b_pw2=_row(b_pw2[0]), g_post=_row(g_post_mix[0])),
        "ffn": [dict(g_pre=_row(g_pre_ffn[i]), w_gate=w_ffn_gate[i].astype(BF16), w_up=w_ffn_up[i].astype(BF16),
                     w_down=w_ffn_down[i].astype(BF16), g_post=_row(g_post_ffn[i])) for i in range(2)],
        "gdn": dict(g_pre=_row(g_pre_mix[1]), w_qkv=w_in[:, :qkv_width].astype(BF16),
                    w_z=w_in[:, qkv_width:qkv_width + v_width].astype(BF16),
                    w_ab=jnp.concatenate([jnp.pad(w_a, lane_pad), jnp.pad(w_b, lane_pad)], axis=1).astype(BF16),
                    w_conv=w_gdn_conv[0], a_log=_pad_lanes(gdn_A_log[0]), dt_bias=_pad_lanes(gdn_dt_bias[0]),
                    g_norm=_row(g_gdn_out_norm[0]), w_out=w_gdn_out[0].astype(BF16), g_post=_row(g_post_mix[1])),
    }

    zeros = lambda shape: jnp.zeros(shape, F32)
    y_prompt, conv_p, s_p, gconv_p = _trunk(
        x_prompt, mod_prompt, zeros((n_prompt,) + cache_conv.shape[2:]), zeros((n_prompt,) + state_gdn.shape[2:]),
        zeros((n_prompt,) + cache_gdn_conv.shape[2:]), w,
        conv_tile=(1, 256), ffn_tile=(1, 512), gdn_tile=(1, 256))
    y_sample, conv_s, s_s, gconv_s = _trunk(
        x_sample, mod_sample, cache_conv[0], state_gdn[0], cache_gdn_conv[0], w,
        conv_tile=(4, 64), ffn_tile=(8, 64), gdn_tile=(4, 64))
    return (y_prompt, y_sample, conv_p, conv_s, s_p, s_s, gconv_p, gconv_s)
```

```python
import functools

import jax
import jax.numpy as jnp
from jax import lax
from jax.experimental import pallas as pl
from jax.experimental.pallas import tpu as pltpu

F32 = jnp.float32
BF16 = jnp.bfloat16
EPS = 1e-6

SUBLANES = 8
LANES = 128
V7X_VMEM_BYTES = 64 * 1024 * 1024
VMEM_LIMIT = V7X_VMEM_BYTES * 7 // 8

CONV_WIDTH = 31
CONV_HIST = CONV_WIDTH - 1
CONV_PAD = 32
SHORT_WIDTH = 4
SHORT_HIST = SHORT_WIDTH - 1
SHORT_PAD = SUBLANES
HEAD_DIM = 128
QK_HEADS = 8
V_HEADS = 16
CHUNK = 64
CONV_ROWS = 16
CONV_COLS = 512
PHASE_A_CHAINS = 16


def _rms(x, g):
    return x * lax.rsqrt(jnp.mean(x * x, axis=-1, keepdims=True) + EPS) * g


def _silu(x):
    return x * jax.nn.sigmoid(x)


def _softplus(x):
    return jnp.maximum(x, 0.0) + jnp.log1p(jnp.exp(-jnp.abs(x)))


def _mm(a, b):
    return jnp.dot(a, b, preferred_element_type=F32)


def _mm_nt(a, b):
    return lax.dot_general(a, b, (((1,), (1,)), ((), ())), preferred_element_type=F32)


def _const_spec(shape):
    zeros = (0,) * len(shape)
    return pl.BlockSpec(shape, lambda *_: zeros, pipeline_mode=pl.Buffered(1))


def _params(n_axes):
    return pltpu.CompilerParams(dimension_semantics=("arbitrary",) * n_axes, vmem_limit_bytes=VMEM_LIMIT)


def _modulation_kernel(c_ref, w_ref, b_ref, o_ref):
    c = c_ref[...]
    o_ref[...] = _mm(_silu(c).astype(BF16), w_ref[...].astype(BF16)) + b_ref[...]


def _modulation(c, w_ada, b_ada):
    depth, d, n = w_ada.shape
    rows = c.shape[0]
    bn = n // 4
    return pl.pallas_call(
        _modulation_kernel,
        grid=(depth, n // bn),
        in_specs=[
            pl.BlockSpec((rows, d), lambda i, j: (0, 0)),
            pl.BlockSpec((None, d, bn), lambda i, j: (i, 0, j)),
            pl.BlockSpec((None, 1, bn), lambda i, j: (i, 0, j)),
        ],
        out_specs=pl.BlockSpec((None, rows, bn), lambda i, j: (i, 0, j)),
        out_shape=jax.ShapeDtypeStruct((depth, rows, n), F32),
        compiler_params=_params(2),
        name="modulation",
    )(c, w_ada, b_ada.reshape(depth, 1, n))


def _conv_mixer_kernel(x_ref, mod_ref, hist_ref, gpre_ref, wpw1_ref, bpw1_ref, wdw_ref, bdw_ref,
                       lng_ref, lnb_ref, wpw2_ref, bpw2_ref, gpost_ref,
                       out_ref, cache_ref, ext_ref, sh_ref, y_ref):
    bt, tt, d = x_ref.shape

    @pl.when(pl.program_id(1) == 0)
    def _():
        ext_ref[:, CONV_PAD - CONV_HIST:CONV_PAD, :] = hist_ref[...]

    x = x_ref[...]
    mod = mod_ref[...]
    shift, scale, gate = mod[:, :, :d], mod[:, :, d:2 * d], mod[:, :, 2 * d:]
    h = _rms(x, gpre_ref[...]) * (1.0 + scale) + shift
    a = _mm(h.reshape(bt * tt, d).astype(BF16), wpw1_ref[...]) + bpw1_ref[...]
    u = a[:, :d] * jax.nn.sigmoid(a[:, d:])
    ext_ref[:, CONV_PAD:CONV_PAD + tt, :] = u.reshape(bt, tt, d)

    for r in range(1, SUBLANES):
        sh_ref[r - 1] = ext_ref[:, r:r + tt + CONV_PAD - SUBLANES, :]

    def row_block(i, carry):
        r0 = pl.multiple_of(i * CONV_ROWS, CONV_ROWS)
        for b in range(bt):
            for c0 in range(0, d, CONV_COLS):
                cols = slice(c0, c0 + CONV_COLS)
                acc = jnp.broadcast_to(bdw_ref[:, cols], (CONV_ROWS, CONV_COLS))
                for k in range(CONV_WIDTH):
                    q, r = divmod(CONV_PAD - CONV_HIST + k, SUBLANES)
                    rows = pl.ds(r0 + q * SUBLANES, CONV_ROWS)
                    tap = ext_ref[b, rows, cols] if r == 0 else sh_ref[r - 1, b, rows, cols]
                    acc = acc + tap * wdw_ref[k:k + 1, cols]
                y_ref[b, pl.ds(r0, CONV_ROWS), cols] = acc
        return carry

    lax.fori_loop(0, tt // CONV_ROWS, row_block, 0)

    y = y_ref[...].reshape(bt * tt, d)
    mu = jnp.mean(y, axis=-1, keepdims=True)
    yc = y - mu
    var = jnp.mean(yc * yc, axis=-1, keepdims=True)
    yn = yc * lax.rsqrt(var + EPS) * lng_ref[...] + lnb_ref[...]
    m = _mm(_silu(yn).astype(BF16), wpw2_ref[...]) + bpw2_ref[...]
    out_ref[...] = x + gate * _rms(m, gpost_ref[...]).reshape(bt, tt, d)

    tail = ext_ref[:, CONV_PAD + tt - CONV_HIST:CONV_PAD + tt, :]
    cache_ref[...] = tail
    ext_ref[:, CONV_PAD - CONV_HIST:CONV_PAD, :] = tail


def _conv_mixer(x, mod, hist, p, *, bt, tt):
    bsz, seq, d = x.shape
    assert bsz % bt == 0 and seq % tt == 0 and tt % CONV_ROWS == 0 and tt >= CONV_HIST
    tok = pl.BlockSpec((bt, tt, d), lambda b, t: (b, t, 0))
    per_batch = lambda rows, cols: pl.BlockSpec((bt, rows, cols), lambda b, t: (b, 0, 0))
    return pl.pallas_call(
        _conv_mixer_kernel,
        grid=(bsz // bt, seq // tt),
        in_specs=[tok, per_batch(1, 3 * d), per_batch(CONV_HIST, d),
                  _const_spec((1, d)), _const_spec((d, 2 * d)), _const_spec((1, 2 * d)),
                  _const_spec((CONV_WIDTH, d)), _const_spec((1, d)), _const_spec((1, d)), _const_spec((1, d)),
                  _const_spec((d, d)), _const_spec((1, d)), _const_spec((1, d))],
        out_specs=[tok, per_batch(CONV_HIST, d)],
        out_shape=[jax.ShapeDtypeStruct(x.shape, F32), jax.ShapeDtypeStruct((bsz, CONV_HIST, d), F32)],
        scratch_shapes=[pltpu.VMEM((bt, CONV_PAD + tt, d), F32),
                        pltpu.VMEM((SUBLANES - 1, bt, CONV_PAD + tt - SUBLANES, d), F32),
                        pltpu.VMEM((bt, tt, d), F32)],
        compiler_params=_params(2),
        name="conv_mixer",
    )(x, mod, hist, p["g_pre"], p["w_pw1"], p["b_pw1"], p["w_dw"], p["b_dw"], p["ln_g"], p["ln_b"],
      p["w_pw2"], p["b_pw2"], p["g_post"])


def _ffn_kernel(x_ref, mod_ref, gpre_ref, wg_ref, wu_ref, wd_ref, gpost_ref, out_ref):
    bt, tt, d = x_ref.shape
    x = x_ref[...]
    mod = mod_ref[...]
    shift, scale, gate = mod[:, :, :d], mod[:, :, d:2 * d], mod[:, :, 2 * d:]
    h = (_rms(x, gpre_ref[...]) * (1.0 + scale) + shift).reshape(bt * tt, d).astype(BF16)
    hid = _silu(_mm(h, wg_ref[...])) * _mm(h, wu_ref[...])
    f = _mm(hid.astype(BF16), wd_ref[...])
    out_ref[...] = x + gate * _rms(f, gpost_ref[...]).reshape(bt, tt, d)


def _ffn(x, mod, p, *, bt, tt):
    bsz, seq, d = x.shape
    f = p["w_gate"].shape[1]
    assert bsz % bt == 0 and seq % tt == 0
    tok = pl.BlockSpec((bt, tt, d), lambda b, t: (b, t, 0))
    return pl.pallas_call(
        _ffn_kernel,
        grid=(bsz // bt, seq // tt),
        in_specs=[tok, pl.BlockSpec((bt, 1, 3 * d), lambda b, t: (b, 0, 1)),
                  _const_spec((1, d)), _const_spec((d, f)), _const_spec((d, f)), _const_spec((f, d)),
                  _const_spec((1, d))],
        out_specs=tok,
        out_shape=jax.ShapeDtypeStruct(x.shape, F32),
        compiler_params=_params(2),
        name="ffn",
    )(x, mod, p["g_pre"], p["w_gate"], p["w_up"], p["w_down"], p["g_post"])


def _unit_lower_inverses(ms):
    n = ms[0].shape[0]
    r = lax.broadcasted_iota(jnp.int32, (n, n), 0)
    c = lax.broadcasted_iota(jnp.int32, (n, n), 1)
    eye = (r == c).astype(F32)
    pair = (r >> 1) == (c >> 1)
    xs = [eye - jnp.where(pair, m, 0.0) for m in ms]
    level = 1
    while (1 << level) < n:
        couple = ((r >> (level + 1)) == (c >> (level + 1))) & ((r >> level) != (c >> level))
        xbs = [x.astype(BF16) for x in xs]
        xcs = [_mm(xb, jnp.where(couple, m, 0.0).astype(BF16)).astype(BF16) for xb, m in zip(xbs, ms)]
        xs = [x - _mm(xc, xb) for x, xc, xb in zip(xs, xcs, xbs)]
        level += 1
    return xs


def _gdn_kernel(x_ref, mod_ref, hist_ref, s0_ref, gpre_ref, wqkv_ref, wz_ref, wab_ref, wconv_ref,
                alog_ref, dtb_ref, gnorm_ref, wout_ref, gpost_ref,
                out_ref, cache_ref, s_ref,
                ext_ref, hb_ref, q_ref, k_ref, v_ref, o_ref, gcb_ref, bb_ref, gr_ref,
                u_ref, w_ref, qd_ref, qkd_ref, kdt_ref, gt_ref):
    bt, tt, d = x_ref.shape
    rows = bt * tt
    n_chunks = tt // CHUNK
    qk_width = QK_HEADS * HEAD_DIM

    @pl.when(pl.program_id(1) == 0)
    def _():
        ext_ref[:, SHORT_PAD - SHORT_HIST:SHORT_PAD, :] = hist_ref[...]
        s_ref[...] = s0_ref[...]

    x = x_ref[...]
    mod = mod_ref[...]
    shift, scale, gate = mod[:, :, :d], mod[:, :, d:2 * d], mod[:, :, 2 * d:]
    hb = (_rms(x, gpre_ref[...]) * (1.0 + scale) + shift).reshape(rows, d).astype(BF16)
    hb_ref[...] = hb

    ext_ref[:, SHORT_PAD:SHORT_PAD + tt, :] = _mm(hb, wqkv_ref[...]).reshape(bt, tt, -1)
    conv = ext_ref[:, SHORT_PAD - SHORT_HIST:SHORT_PAD - SHORT_HIST + tt, :] * wconv_ref[0:1, :]
    for k in range(1, SHORT_WIDTH):
        lo = SHORT_PAD - SHORT_HIST + k
        conv = conv + ext_ref[:, lo:lo + tt, :] * wconv_ref[k:k + 1, :]
    qkv = _silu(conv)
    tail = ext_ref[:, SHORT_PAD + tt - SHORT_HIST:SHORT_PAD + tt, :]
    cache_ref[...] = tail
    ext_ref[:, SHORT_PAD - SHORT_HIST:SHORT_PAD, :] = tail

    for b in range(bt):
        for j in range(QK_HEADS):
            qh = qkv[b, :, j * HEAD_DIM:(j + 1) * HEAD_DIM]
            kh = qkv[b, :, qk_width + j * HEAD_DIM:qk_width + (j + 1) * HEAD_DIM]
            q_ref[b * QK_HEADS + j] = qh * (lax.rsqrt(jnp.sum(qh * qh, axis=-1, keepdims=True) + EPS)
                                            * (HEAD_DIM ** -0.5))
            k_ref[b * QK_HEADS + j] = kh * lax.rsqrt(jnp.sum(kh * kh, axis=-1, keepdims=True) + EPS)
        for hh in range(V_HEADS):
            lo = 2 * qk_width + hh * HEAD_DIM
            v_ref[b * V_HEADS + hh] = qkv[b, :, lo:lo + HEAD_DIM]

    ab = _mm(hb, wab_ref[...])
    g = -jnp.exp(alog_ref[...]) * _softplus(ab[:, :LANES] + dtb_ref[...])
    beta = jax.nn.sigmoid(ab[:, LANES:])
    r_i = lax.broadcasted_iota(jnp.int32, (rows, rows), 0)
    c_i = lax.broadcasted_iota(jnp.int32, (rows, rows), 1)
    tri = ((r_i >= c_i) & (r_i // CHUNK == c_i // CHUNK)).astype(F32)
    gc = jnp.dot(tri, g, precision=lax.Precision.HIGHEST, preferred_element_type=F32)
    pad = (-rows) % LANES
    gc_t = (jnp.concatenate([gc, jnp.zeros((pad, LANES), F32)], axis=0) if pad else gc).T
    for b in range(bt):
        for hh in range(V_HEADS):
            n = b * V_HEADS + hh
            gcb_ref[n] = jnp.broadcast_to(gc[b * tt:(b + 1) * tt, hh:hh + 1], (tt, LANES))
            bb_ref[n] = jnp.broadcast_to(beta[b * tt:(b + 1) * tt, hh:hh + 1], (tt, LANES))
            gr_ref[n] = gc_t[hh:hh + 1, b * tt:(b + 1) * tt]

    r_c = lax.broadcasted_iota(jnp.int32, (CHUNK, CHUNK), 0)
    c_c = lax.broadcasted_iota(jnp.int32, (CHUNK, CHUNK), 1)
    incl = r_c >= c_c
    strict = r_c > c_c
    zeros_half = jnp.zeros((CHUNK, HEAD_DIM), F32)

    pairs_per_iter = max(1, PHASE_A_CHAINS // (2 * n_chunks))
    assert (bt * QK_HEADS) % pairs_per_iter == 0

    def phase_a(it, carry):
        items = [(it * pairs_per_iter + p, c) for p in range(pairs_per_iter) for c in range(n_chunks)]
        rows_of = lambda c: slice(c * CHUNK, (c + 1) * CHUNK)
        qs = [q_ref[n, rows_of(c), :] for n, c in items]
        ks = [k_ref[n, rows_of(c), :] for n, c in items]
        prods = [_mm_nt(jnp.concatenate([q.astype(BF16), k.astype(BF16)], axis=0), k.astype(BF16))
                 for q, k in zip(qs, ks)]
        k_ts = [jnp.concatenate([k, zeros_half], axis=0).T[:, :CHUNK] for k in ks]
        heads = [(i, 2 * n + hh, c) for i, (n, c) in enumerate(items) for hh in range(2)]
        gcbs = [gcb_ref[vh, rows_of(c), :] for _, vh, c in heads]
        bbs = [bb_ref[vh, rows_of(c), :] for _, vh, c in heads]
        g_rows = [gr_ref[vh][:, rows_of(c)] for _, vh, c in heads]
        decays = [jnp.where(incl, jnp.exp(jnp.where(incl, gcb[:, :CHUNK] - g_row, 0.0)), 0.0)
                  for gcb, g_row in zip(gcbs, g_rows)]
        t_invs = _unit_lower_inverses([jnp.where(strict, prods[i][CHUNK:] * bb[:, :CHUNK] * decay, 0.0)
                                       for (i, _, _), bb, decay in zip(heads, bbs, decays)])
        egs = [jnp.exp(gcb) for gcb in gcbs]
        sols = [_mm(t_inv.astype(BF16),
                    jnp.concatenate([v_ref[vh, rows_of(c), :] * bb, ks[i] * (bb * eg)], axis=1).astype(BF16))
                for (i, vh, c), t_inv, bb, eg in zip(heads, t_invs, bbs, egs)]
        for (i, vh, c), sol, eg, decay, gcb, g_row in zip(heads, sols, egs, decays, gcbs, g_rows):
            rs = rows_of(c)
            g_last = gcb[CHUNK - 1:CHUNK, :]
            u_ref[vh, rs, :] = sol[:, :HEAD_DIM]
            w_ref[vh, rs, :] = sol[:, HEAD_DIM:].astype(BF16)
            qd_ref[vh, rs, :] = (qs[i] * eg).astype(BF16)
            qkd_ref[vh, rs, :] = (prods[i][:CHUNK] * decay).astype(BF16)
            kdt_ref[vh * n_chunks + c] = (k_ts[i] * jnp.exp(g_last[:, :CHUNK] - g_row)).astype(BF16)
            gt_ref[vh * n_chunks + c] = jnp.exp(g_last)
        return carry

    lax.fori_loop(0, bt * QK_HEADS // pairs_per_iter, phase_a, 0)

    all_heads = range(bt * V_HEADS)
    for c in range(n_chunks):
        rs = slice(c * CHUNK, (c + 1) * CHUNK)
        ss = [s_ref[vh] for vh in all_heads]
        ws_qs = [_mm(jnp.concatenate([w_ref[vh, rs, :], qd_ref[vh, rs, :]], axis=0), ss[vh].astype(BF16))
                 for vh in all_heads]
        v_news = [(u_ref[vh, rs, :] - ws_qs[vh][:CHUNK]).astype(BF16) for vh in all_heads]
        for vh in all_heads:
            o_ref[vh, rs, :] = ws_qs[vh][CHUNK:] + _mm(qkd_ref[vh, rs, :], v_news[vh])
        for vh in all_heads:
            s_ref[vh] = ss[vh] * gt_ref[vh * n_chunks + c] + _mm(kdt_ref[vh * n_chunks + c], v_news[vh])

    z = _mm(hb_ref[...], wz_ref[...])
    gated = []
    for hh in range(V_HEADS):
        o_h = jnp.concatenate([o_ref[b * V_HEADS + hh] for b in range(bt)], axis=0)
        gated.append((_rms(o_h, gnorm_ref[...]) * _silu(z[:, hh * HEAD_DIM:(hh + 1) * HEAD_DIM])).astype(BF16))
    m = _mm(jnp.concatenate(gated, axis=1), wout_ref[...])
    out_ref[...] = x + gate * _rms(m, gpost_ref[...]).reshape(bt, tt, d)


def _gdn_mixer(x, mod, hist, s0, p, *, bt, tt):
    bsz, seq, d = x.shape
    ch = hist.shape[-1]
    v_width = V_HEADS * HEAD_DIM
    assert bsz % bt == 0 and seq % tt == 0 and tt % CHUNK == 0
    tok = pl.BlockSpec((bt, tt, d), lambda b, t: (b, t, 0))
    per_batch = lambda rows, cols: pl.BlockSpec((bt, rows, cols), lambda b, t: (b, 0, 0))
    state = pl.BlockSpec((bt * V_HEADS, HEAD_DIM, HEAD_DIM), lambda b, t: (b, 0, 0))
    heads = lambda n: pltpu.VMEM((bt * n, tt, HEAD_DIM), F32)
    return pl.pallas_call(
        _gdn_kernel,
        grid=(bsz // bt, seq // tt),
        in_specs=[tok, per_batch(1, 3 * d), per_batch(SHORT_HIST, ch), state,
                  _const_spec((1, d)), _const_spec((d, ch)), _const_spec((d, v_width)), _const_spec((d, 2 * LANES)),
                  _const_spec((SHORT_WIDTH, ch)), _const_spec((1, LANES)), _const_spec((1, LANES)),
                  _const_spec((1, HEAD_DIM)), _const_spec((v_width, d)), _const_spec((1, d))],
        out_specs=[tok, per_batch(SHORT_HIST, ch), state],
        out_shape=[jax.ShapeDtypeStruct(x.shape, F32), jax.ShapeDtypeStruct((bsz, SHORT_HIST, ch), F32),
                   jax.ShapeDtypeStruct(s0.shape, F32)],
        scratch_shapes=[pltpu.VMEM((bt, SHORT_PAD + tt, ch), F32), pltpu.VMEM((bt * tt, d), BF16),
                        heads(QK_HEADS), heads(QK_HEADS), heads(V_HEADS), heads(V_HEADS),
                        heads(V_HEADS), heads(V_HEADS), pltpu.VMEM((bt * V_HEADS, 1, tt), F32),
                        heads(V_HEADS), pltpu.VMEM((bt * V_HEADS, tt, HEAD_DIM), BF16),
                        pltpu.VMEM((bt * V_HEADS, tt, HEAD_DIM), BF16), pltpu.VMEM((bt * V_HEADS, tt, CHUNK), BF16),
                        pltpu.VMEM((bt * V_HEADS * (tt // CHUNK), HEAD_DIM, CHUNK), BF16),
                        pltpu.VMEM((bt * V_HEADS * (tt // CHUNK), 1, HEAD_DIM), F32)],
        compiler_params=_params(2),
        name="gdn_mixer",
    )(x, mod, hist, s0, p["g_pre"], p["w_qkv"], p["w_z"], p["w_ab"], p["w_conv"], p["a_log"], p["dt_bias"],
      p["g_norm"], p["w_out"], p["g_post"])


def _row(v):
    return v.reshape(1, -1)


def _pad_lanes(v):
    return jnp.pad(v.reshape(1, -1), ((0, 0), (0, LANES - v.shape[-1])))


def _trunk(x, mod, conv_hist, s0, gdn_hist, w, *, conv_tile, ffn_tile, gdn_tile):
    bsz = x.shape[0]
    x, conv_cache = _conv_mixer(x, mod[0], conv_hist, w["conv"], bt=conv_tile[0], tt=conv_tile[1])
    x = _ffn(x, mod[0], w["ffn"][0], bt=ffn_tile[0], tt=ffn_tile[1])
    x, gdn_cache, s = _gdn_mixer(x, mod[1], gdn_hist, s0.reshape(bsz * V_HEADS, HEAD_DIM, HEAD_DIM), w["gdn"],
                                 bt=gdn_tile[0], tt=gdn_tile[1])
    x = _ffn(x, mod[1], w["ffn"][1], bt=ffn_tile[0], tt=ffn_tile[1])
    return x, conv_cache[None], s.reshape((1,) + s0.shape), gdn_cache[None]


def kernel(x_prompt, x_sample, c_prompt, c_sample, cache_conv, state_gdn, cache_gdn_conv, w_ada, b_ada, g_pre_mix, g_post_mix, g_pre_ffn, g_post_ffn, w_ffn_gate, w_ffn_up, w_ffn_down, w_pw1, b_pw1, w_dw, b_dw, ln_conv_g, ln_conv_b, w_pw2, b_pw2, w_gdn_in, w_gdn_conv, gdn_A_log, gdn_dt_bias, g_gdn_out_norm, w_gdn_out):
    n_prompt = x_prompt.shape[0]
    d = x_prompt.shape[-1]
    qkv_width = 2 * QK_HEADS * HEAD_DIM + V_HEADS * HEAD_DIM
    v_width = V_HEADS * HEAD_DIM

    mod = _modulation(jnp.concatenate([c_prompt, c_sample], axis=0), w_ada, b_ada)[:, :, None, :]
    mod_prompt, mod_sample = mod[:, :n_prompt], mod[:, n_prompt:]

    w_in = w_gdn_in[0]
    w_a = w_in[:, qkv_width + v_width:qkv_width + v_width + V_HEADS]
    w_b = w_in[:, qkv_width + v_width + V_HEADS:]
    lane_pad = ((0, 0), (0, LANES - V_HEADS))
    w = {
        "conv": dict(g_pre=_row(g_pre_mix[0]), w_pw1=w_pw1[0].astype(BF16), b_pw1=_row(b_pw1[0]), w_dw=w_dw[0],
                     b_dw=_row(b_dw[0]), ln_g=_row(ln_conv_g[0]), ln_b=_row(ln_conv_b[0]),
                     w_pw2=w_pw2[0].astype(BF16), b_pw2=_row(b_pw2[0]), g_post=_row(g_post_mix[0])),
        "ffn": [dict(g_pre=_row(g_pre_ffn[i]), w_gate=w_ffn_gate[i].astype(BF16), w_up=w_ffn_up[i].astype(BF16),
                     w_down=w_ffn_down[i].astype(BF16), g_post=_row(g_post_ffn[i])) for i in range(2)],
        "gdn": dict(g_pre=_row(g_pre_mix[1]), w_qkv=w_in[:, :qkv_width].astype(BF16),
                    w_z=w_in[:, qkv_width:qkv_width + v_width].astype(BF16),
                    w_ab=jnp.concatenate([jnp.pad(w_a, lane_pad), jnp.pad(w_b, lane_pad)], axis=1).astype(BF16),
                    w_conv=w_gdn_conv[0], a_log=_pad_lanes(gdn_A_log[0]), dt_bias=_pad_lanes(gdn_dt_bias[0]),
                    g_norm=_row(g_gdn_out_norm[0]), w_out=w_gdn_out[0].astype(BF16), g_post=_row(g_post_mix[1])),
    }

    zeros = lambda shape: jnp.zeros(shape, F32)
    y_prompt, conv_p, s_p, gconv_p = _trunk(
        x_prompt, mod_prompt, zeros((n_prompt,) + cache_conv.shape[2:]), zeros((n_prompt,) + state_gdn.shape[2:]),
        zeros((n_prompt,) + cache_gdn_conv.shape[2:]), w,
        conv_tile=(1, 256), ffn_tile=(1, 512), gdn_tile=(1, 256))
    y_sample, conv_s, s_s, gconv_s = _trunk(
        x_sample, mod_sample, cache_conv[0], state_gdn[0], cache_gdn_conv[0], w,
        conv_tile=(4, 64), ffn_tile=(8, 64), gdn_tile=(2, 64))
    return (y_prompt, y_sample, conv_p, conv_s, s_p, s_s, gconv_p, gconv_s)
```

```python
import functools

import jax
import jax.numpy as jnp
from jax import lax
from jax.experimental import pallas as pl
from jax.experimental.pallas import tpu as pltpu

F32 = jnp.float32
BF16 = jnp.bfloat16
EPS = 1e-6

SUBLANES = 8
LANES = 128
V7X_VMEM_BYTES = 64 * 1024 * 1024
VMEM_LIMIT = V7X_VMEM_BYTES * 7 // 8

CONV_WIDTH = 31
CONV_HIST = CONV_WIDTH - 1
CONV_PAD = 32
SHORT_WIDTH = 4
SHORT_HIST = SHORT_WIDTH - 1
SHORT_PAD = SUBLANES
HEAD_DIM = 128
QK_HEADS = 8
V_HEADS = 16
CHUNK = 64
CONV_ROWS = 64
CONV_COLS = 256
PACK = 2


def _rms(x, g):
    return x * lax.rsqrt(jnp.mean(x * x, axis=-1, keepdims=True) + EPS) * g


def _silu(x):
    return x * jax.nn.sigmoid(x)


def _softplus(x):
    return jnp.maximum(x, 0.0) + jnp.log1p(jnp.exp(-jnp.abs(x)))


def _mm(a, b):
    return jnp.dot(a, b, preferred_element_type=F32)


def _mm_nt(a, b):
    return lax.dot_general(a, b, (((1,), (1,)), ((), ())), preferred_element_type=F32)


def _const_spec(shape):
    zeros = (0,) * len(shape)
    return pl.BlockSpec(shape, lambda *_: zeros, pipeline_mode=pl.Buffered(1))


def _params(n_axes):
    return pltpu.CompilerParams(dimension_semantics=("arbitrary",) * n_axes, vmem_limit_bytes=VMEM_LIMIT)


def _modulation_kernel(c_ref, w_ref, b_ref, o_ref):
    c = c_ref[...]
    o_ref[...] = _mm(_silu(c).astype(BF16), w_ref[...].astype(BF16)) + b_ref[...]


def _modulation(c, w_ada, b_ada):
    depth, d, n = w_ada.shape
    rows = c.shape[0]
    bn = n // 4
    return pl.pallas_call(
        _modulation_kernel,
        grid=(depth, n // bn),
        in_specs=[
            pl.BlockSpec((rows, d), lambda i, j: (0, 0)),
            pl.BlockSpec((None, d, bn), lambda i, j: (i, 0, j)),
            pl.BlockSpec((None, 1, bn), lambda i, j: (i, 0, j)),
        ],
        out_specs=pl.BlockSpec((None, rows, bn), lambda i, j: (i, 0, j)),
        out_shape=jax.ShapeDtypeStruct((depth, rows, n), F32),
        compiler_params=_params(2),
        name="modulation",
    )(c, w_ada, b_ada.reshape(depth, 1, n))


def _conv_mixer_kernel(x_ref, mod_ref, hist_ref, gpre_ref, wpw1_ref, bpw1_ref, wdw_ref, bdw_ref,
                       lng_ref, lnb_ref, wpw2_ref, bpw2_ref, gpost_ref,
                       out_ref, cache_ref, ext_ref, sh_ref, y_ref, wb_ref):
    bt, tt, d = x_ref.shape

    @pl.when(pl.program_id(1) == 0)
    def _():
        ext_ref[:, CONV_PAD - CONV_HIST:CONV_PAD, :] = hist_ref[...]

    x = x_ref[...]
    mod = mod_ref[...]
    shift, scale, gate = mod[:, :, :d], mod[:, :, d:2 * d], mod[:, :, 2 * d:]
    h = _rms(x, gpre_ref[...]) * (1.0 + scale) + shift
    a = _mm(h.reshape(bt * tt, d).astype(BF16), wpw1_ref[...]) + bpw1_ref[...]
    u = a[:, :d] * jax.nn.sigmoid(a[:, d:])
    ext_ref[:, CONV_PAD:CONV_PAD + tt, :] = u.reshape(bt, tt, d)

    for r in range(1, SUBLANES):
        sh_ref[r - 1] = ext_ref[:, r:r + tt + CONV_PAD - SUBLANES, :]

    @pl.when((pl.program_id(0) == 0) & (pl.program_id(1) == 0))
    def _():
        for k in range(CONV_WIDTH):
            wb_ref[k] = jnp.broadcast_to(wdw_ref[k:k + 1, :], (SUBLANES, d))

    def row_block(i, carry):
        r0 = pl.multiple_of(i * CONV_ROWS, CONV_ROWS)
        for b in range(bt):
            for c0 in range(0, d, CONV_COLS):
                cols = slice(c0, c0 + CONV_COLS)
                bias = jnp.broadcast_to(bdw_ref[:, cols], (SUBLANES, CONV_COLS))
                accs = [bias] * (CONV_ROWS // SUBLANES)
                for k in range(CONV_WIDTH):
                    q, r = divmod(CONV_PAD - CONV_HIST + k, SUBLANES)
                    w_k = wb_ref[k, :, cols]
                    for j in range(len(accs)):
                        rows = pl.ds(r0 + (q + j) * SUBLANES, SUBLANES)
                        tap = ext_ref[b, rows, cols] if r == 0 else sh_ref[r - 1, b, rows, cols]
                        accs[j] = accs[j] + tap * w_k
                for j, acc in enumerate(accs):
                    y_ref[b, pl.ds(r0 + j * SUBLANES, SUBLANES), cols] = acc
        return carry

    lax.fori_loop(0, tt // CONV_ROWS, row_block, 0)

    y = y_ref[...].reshape(bt * tt, d)
    mu = jnp.mean(y, axis=-1, keepdims=True)
    yc = y - mu
    var = jnp.mean(yc * yc, axis=-1, keepdims=True)
    yn = yc * lax.rsqrt(var + EPS) * lng_ref[...] + lnb_ref[...]
    m = _mm(_silu(yn).astype(BF16), wpw2_ref[...]) + bpw2_ref[...]
    out_ref[...] = x + gate * _rms(m, gpost_ref[...]).reshape(bt, tt, d)

    tail = ext_ref[:, CONV_PAD + tt - CONV_HIST:CONV_PAD + tt, :]
    cache_ref[...] = tail
    ext_ref[:, CONV_PAD - CONV_HIST:CONV_PAD, :] = tail


def _conv_mixer(x, mod, hist, p, *, bt, tt):
    bsz, seq, d = x.shape
    assert bsz % bt == 0 and seq % tt == 0 and tt % CONV_ROWS == 0 and tt >= CONV_HIST
    tok = pl.BlockSpec((bt, tt, d), lambda b, t: (b, t, 0))
    per_batch = lambda rows, cols: pl.BlockSpec((bt, rows, cols), lambda b, t: (b, 0, 0))
    return pl.pallas_call(
        _conv_mixer_kernel,
        grid=(bsz // bt, seq // tt),
        in_specs=[tok, per_batch(1, 3 * d), per_batch(CONV_HIST, d),
                  _const_spec((1, d)), _const_spec((d, 2 * d)), _const_spec((1, 2 * d)),
                  _const_spec((CONV_WIDTH, d)), _const_spec((1, d)), _const_spec((1, d)), _const_spec((1, d)),
                  _const_spec((d, d)), _const_spec((1, d)), _const_spec((1, d))],
        out_specs=[tok, per_batch(CONV_HIST, d)],
        out_shape=[jax.ShapeDtypeStruct(x.shape, F32), jax.ShapeDtypeStruct((bsz, CONV_HIST, d), F32)],
        scratch_shapes=[pltpu.VMEM((bt, CONV_PAD + tt, d), F32),
                        pltpu.VMEM((SUBLANES - 1, bt, CONV_PAD + tt - SUBLANES, d), F32),
                        pltpu.VMEM((bt, tt, d), F32), pltpu.VMEM((CONV_WIDTH, SUBLANES, d), F32)],
        compiler_params=_params(2),
        name="conv_mixer",
    )(x, mod, hist, p["g_pre"], p["w_pw1"], p["b_pw1"], p["w_dw"], p["b_dw"], p["ln_g"], p["ln_b"],
      p["w_pw2"], p["b_pw2"], p["g_post"])


def _ffn_kernel(x_ref, mod_ref, gpre_ref, wg_ref, wu_ref, wd_ref, gpost_ref, out_ref):
    bt, tt, d = x_ref.shape
    x = x_ref[...]
    mod = mod_ref[...]
    shift, scale, gate = mod[:, :, :d], mod[:, :, d:2 * d], mod[:, :, 2 * d:]
    h = (_rms(x, gpre_ref[...]) * (1.0 + scale) + shift).reshape(bt * tt, d).astype(BF16)
    hid = _silu(_mm(h, wg_ref[...])) * _mm(h, wu_ref[...])
    f = _mm(hid.astype(BF16), wd_ref[...])
    out_ref[...] = x + gate * _rms(f, gpost_ref[...]).reshape(bt, tt, d)


def _ffn(x, mod, p, *, bt, tt):
    bsz, seq, d = x.shape
    f = p["w_gate"].shape[1]
    assert bsz % bt == 0 and seq % tt == 0
    tok = pl.BlockSpec((bt, tt, d), lambda b, t: (b, t, 0))
    return pl.pallas_call(
        _ffn_kernel,
        grid=(bsz // bt, seq // tt),
        in_specs=[tok, pl.BlockSpec((bt, 1, 3 * d), lambda b, t: (b, 0, 1)),
                  _const_spec((1, d)), _const_spec((d, f)), _const_spec((d, f)), _const_spec((f, d)),
                  _const_spec((1, d))],
        out_specs=tok,
        out_shape=jax.ShapeDtypeStruct(x.shape, F32),
        compiler_params=_params(2),
        name="ffn",
    )(x, mod, p["g_pre"], p["w_gate"], p["w_up"], p["w_down"], p["g_post"])


def _packed_unit_lower_inverses(ms, bd_mask):
    n, width = ms[0].shape
    r = lax.broadcasted_iota(jnp.int32, (n, width), 0)
    c = lax.broadcasted_iota(jnp.int32, (n, width), 1) & (n - 1)
    block_diag = lambda y: jnp.concatenate([y] * (width // n), axis=0) * bd_mask
    xs = [(r == c).astype(F32) - jnp.where((r >> 1) == (c >> 1), m, 0.0) for m in ms]
    level = 1
    while (1 << level) < n:
        couple = ((r >> (level + 1)) == (c >> (level + 1))) & ((r >> level) != (c >> level))
        xbs = [x.astype(BF16) for x in xs]
        xcs = [_mm(xb, block_diag(jnp.where(couple, m, 0.0).astype(BF16))).astype(BF16) for xb, m in zip(xbs, ms)]
        xs = [x - _mm(xc, block_diag(xb)) for x, xc, xb in zip(xs, xcs, xbs)]
        level += 1
    return xs


def _gdn_kernel(x_ref, mod_ref, hist_ref, s0_ref, gpre_ref, wqkv_ref, wz_ref, wab_ref, wconv_ref,
                alog_ref, dtb_ref, gnorm_ref, wout_ref, gpost_ref,
                out_ref, cache_ref, s_ref,
                ext_ref, hb_ref, q_ref, k_ref, v_ref, o_ref, gcb_ref, bb_ref, gr_ref,
                u_ref, w_ref, qd_ref, qkd_ref, kdt_ref, gt_ref):
    bt, tt, d = x_ref.shape
    rows = bt * tt
    n_chunks = tt // CHUNK
    qk_width = QK_HEADS * HEAD_DIM

    @pl.when(pl.program_id(1) == 0)
    def _():
        ext_ref[:, SHORT_PAD - SHORT_HIST:SHORT_PAD, :] = hist_ref[...]
        s_ref[...] = s0_ref[...]

    x = x_ref[...]
    mod = mod_ref[...]
    shift, scale, gate = mod[:, :, :d], mod[:, :, d:2 * d], mod[:, :, 2 * d:]
    hb = (_rms(x, gpre_ref[...]) * (1.0 + scale) + shift).reshape(rows, d).astype(BF16)
    hb_ref[...] = hb

    ext_ref[:, SHORT_PAD:SHORT_PAD + tt, :] = _mm(hb, wqkv_ref[...]).reshape(bt, tt, -1)
    conv = ext_ref[:, SHORT_PAD - SHORT_HIST:SHORT_PAD - SHORT_HIST + tt, :] * wconv_ref[0:1, :]
    for k in range(1, SHORT_WIDTH):
        lo = SHORT_PAD - SHORT_HIST + k
        conv = conv + ext_ref[:, lo:lo + tt, :] * wconv_ref[k:k + 1, :]
    qkv = _silu(conv)
    tail = ext_ref[:, SHORT_PAD + tt - SHORT_HIST:SHORT_PAD + tt, :]
    cache_ref[...] = tail
    ext_ref[:, SHORT_PAD - SHORT_HIST:SHORT_PAD, :] = tail

    for b in range(bt):
        for j in range(QK_HEADS):
            qh = qkv[b, :, j * HEAD_DIM:(j + 1) * HEAD_DIM]
            kh = qkv[b, :, qk_width + j * HEAD_DIM:qk_width + (j + 1) * HEAD_DIM]
            q_ref[b * QK_HEADS + j] = qh * (lax.rsqrt(jnp.sum(qh * qh, axis=-1, keepdims=True) + EPS)
                                            * (HEAD_DIM ** -0.5))
            k_ref[b * QK_HEADS + j] = kh * lax.rsqrt(jnp.sum(kh * kh, axis=-1, keepdims=True) + EPS)
        for hh in range(V_HEADS):
            lo = 2 * qk_width + hh * HEAD_DIM
            v_ref[b * V_HEADS + hh] = qkv[b, :, lo:lo + HEAD_DIM]

    ab = _mm(hb, wab_ref[...])
    g = -jnp.exp(alog_ref[...]) * _softplus(ab[:, :LANES] + dtb_ref[...])
    beta = jax.nn.sigmoid(ab[:, LANES:])
    r_i = lax.broadcasted_iota(jnp.int32, (rows, rows), 0)
    c_i = lax.broadcasted_iota(jnp.int32, (rows, rows), 1)
    tri = ((r_i >= c_i) & (r_i // CHUNK == c_i // CHUNK)).astype(F32)
    gc = jnp.dot(tri, g, precision=lax.Precision.HIGHEST, preferred_element_type=F32)
    pad = (-rows) % LANES
    gc_t = (jnp.concatenate([gc, jnp.zeros((pad, LANES), F32)], axis=0) if pad else gc).T
    for b in range(bt):
        for hh in range(V_HEADS):
            n = b * V_HEADS + hh
            gcb_ref[n] = jnp.broadcast_to(gc[b * tt:(b + 1) * tt, hh:hh + 1], (tt, LANES))
            bb_ref[n] = jnp.broadcast_to(beta[b * tt:(b + 1) * tt, hh:hh + 1], (tt, LANES))
            gr_ref[n] = gc_t[hh:hh + 1, b * tt:(b + 1) * tt]

    width = PACK * CHUNK
    r_c = lax.broadcasted_iota(jnp.int32, (CHUNK, width), 0)
    c_c = lax.broadcasted_iota(jnp.int32, (CHUNK, width), 1) & (CHUNK - 1)
    incl = r_c >= c_c
    strict = r_c > c_c
    low_half = lax.broadcasted_iota(jnp.int32, (CHUNK, HEAD_DIM), 1) < CHUNK
    bd_mask = ((lax.broadcasted_iota(jnp.int32, (width, width), 0) // CHUNK)
               == (lax.broadcasted_iota(jnp.int32, (width, width), 1) // CHUNK)).astype(BF16)
    block_diag = lambda y: jnp.concatenate([y] * PACK, axis=0) * bd_mask
    zeros_half = jnp.zeros((CHUNK, HEAD_DIM), F32)
    rows_of = lambda c: slice(c * CHUNK, (c + 1) * CHUNK)

    packs = [(g, c) for g in range(bt * V_HEADS // PACK) for c in range(n_chunks)]
    pairs = [(g * (PACK // 2) + j, c) for g, c in packs for j in range(PACK // 2)]
    heads = [(i, 2 * n + hh, c) for i, (n, c) in enumerate(pairs) for hh in range(2)]
    qs = [q_ref[n, rows_of(c), :] for n, c in pairs]
    ks = [k_ref[n, rows_of(c), :] for n, c in pairs]
    kbs = [k.astype(BF16) for k in ks]
    prods = [_mm_nt(jnp.concatenate([q.astype(BF16), kb], axis=0), jnp.concatenate([kb, kb], axis=0))
             for q, kb in zip(qs, kbs)]
    k_ts = [jnp.concatenate([k, zeros_half], axis=0).T[:, :CHUNK] for k in ks]
    gcbs = [gcb_ref[vh, rows_of(c), :] for _, vh, c in heads]
    bbs = [bb_ref[vh, rows_of(c), :] for _, vh, c in heads]
    g_rows = [gr_ref[vh][:, rows_of(c)] for _, vh, c in heads]
    side_by_side = lambda per_pair, p: jnp.concatenate(per_pair[p * (PACK // 2):(p + 1) * (PACK // 2)], axis=1)
    pair_lanes = lambda vals: [jnp.where(low_half, vals[2 * i], vals[2 * i + 1]) for i in range(len(pairs))]
    gcb_pairs, bb_pairs = pair_lanes(gcbs), pair_lanes(bbs)
    g_row_pairs = [jnp.concatenate([g_rows[2 * i], g_rows[2 * i + 1]], axis=1) for i in range(len(pairs))]
    decays, m_packs = [], []
    for p in range(len(packs)):
        diff = side_by_side(gcb_pairs, p) - side_by_side(g_row_pairs, p)
        decay = jnp.where(incl, jnp.exp(jnp.where(incl, diff, 0.0)), 0.0)
        a_kk = side_by_side([pr[CHUNK:] for pr in prods], p)
        decays.append(decay)
        m_packs.append(jnp.where(strict, a_kk * side_by_side(bb_pairs, p) * decay, 0.0))
    t_invs = _packed_unit_lower_inverses(m_packs, bd_mask)
    egs = [jnp.exp(gcb) for gcb in gcbs]
    rhss = [jnp.concatenate([v_ref[vh, rows_of(c), :] * bb, ks[i] * (bb * eg)], axis=1).astype(BF16)
            for (i, vh, c), bb, eg in zip(heads, bbs, egs)]
    sols = [_mm(block_diag(t_inv.astype(BF16)), jnp.concatenate(rhss[p * PACK:(p + 1) * PACK], axis=0))
            for p, t_inv in enumerate(t_invs)]
    for p, (g, c) in enumerate(packs):
        a_qk = side_by_side([pr[:CHUNK] for pr in prods], p)
        qkd_ref[g * n_chunks + c] = block_diag((a_qk * decays[p]).astype(BF16))
    for h, ((i, vh, c), eg, gcb, g_row) in enumerate(zip(heads, egs, gcbs, g_rows)):
        rs = rows_of(c)
        sol = sols[h // PACK][(h % PACK) * CHUNK:(h % PACK + 1) * CHUNK]
        g_last = gcb[CHUNK - 1:CHUNK, :]
        u_ref[vh, rs, :] = sol[:, :HEAD_DIM]
        w_ref[vh, rs, :] = sol[:, HEAD_DIM:].astype(BF16)
        qd_ref[vh, rs, :] = (qs[i] * eg).astype(BF16)
        kdt_ref[vh * n_chunks + c] = (k_ts[i] * jnp.exp(g_last[:, :CHUNK] - g_row)).astype(BF16)
        gt_ref[vh * n_chunks + c] = jnp.exp(g_last)

    all_heads = range(bt * V_HEADS)
    for c in range(n_chunks):
        rs = rows_of(c)
        ss = [s_ref[vh] for vh in all_heads]
        ws_qs = [_mm(jnp.concatenate([w_ref[vh, rs, :], qd_ref[vh, rs, :]], axis=0), ss[vh].astype(BF16))
                 for vh in all_heads]
        v_news = [(u_ref[vh, rs, :] - ws_qs[vh][:CHUNK]).astype(BF16) for vh in all_heads]
        for g in range(bt * V_HEADS // PACK):
            qk_v = _mm(qkd_ref[g * n_chunks + c], jnp.concatenate(v_news[g * PACK:(g + 1) * PACK], axis=0))
            for j in range(PACK):
                vh = g * PACK + j
                o_ref[vh, rs, :] = ws_qs[vh][CHUNK:] + qk_v[j * CHUNK:(j + 1) * CHUNK]
        for vh in all_heads:
            s_ref[vh] = ss[vh] * gt_ref[vh * n_chunks + c] + _mm(kdt_ref[vh * n_chunks + c], v_news[vh])

    z = _mm(hb_ref[...], wz_ref[...])
    gated = []
    for hh in range(V_HEADS):
        o_h = jnp.concatenate([o_ref[b * V_HEADS + hh] for b in range(bt)], axis=0)
        gated.append((_rms(o_h, gnorm_ref[...]) * _silu(z[:, hh * HEAD_DIM:(hh + 1) * HEAD_DIM])).astype(BF16))
    m = _mm(jnp.concatenate(gated, axis=1), wout_ref[...])
    out_ref[...] = x + gate * _rms(m, gpost_ref[...]).reshape(bt, tt, d)


def _gdn_mixer(x, mod, hist, s0, p, *, bt, tt):
    bsz, seq, d = x.shape
    ch = hist.shape[-1]
    v_width = V_HEADS * HEAD_DIM
    assert bsz % bt == 0 and seq % tt == 0 and tt % CHUNK == 0
    tok = pl.BlockSpec((bt, tt, d), lambda b, t: (b, t, 0))
    per_batch = lambda rows, cols: pl.BlockSpec((bt, rows, cols), lambda b, t: (b, 0, 0))
    state = pl.BlockSpec((bt * V_HEADS, HEAD_DIM, HEAD_DIM), lambda b, t: (b, 0, 0))
    heads = lambda n: pltpu.VMEM((bt * n, tt, HEAD_DIM), F32)
    return pl.pallas_call(
        _gdn_kernel,
        grid=(bsz // bt, seq // tt),
        in_specs=[tok, per_batch(1, 3 * d), per_batch(SHORT_HIST, ch), state,
                  _const_spec((1, d)), _const_spec((d, ch)), _const_spec((d, v_width)), _const_spec((d, 2 * LANES)),
                  _const_spec((SHORT_WIDTH, ch)), _const_spec((1, LANES)), _const_spec((1, LANES)),
                  _const_spec((1, HEAD_DIM)), _const_spec((v_width, d)), _const_spec((1, d))],
        out_specs=[tok, per_batch(SHORT_HIST, ch), state],
        out_shape=[jax.ShapeDtypeStruct(x.shape, F32), jax.ShapeDtypeStruct((bsz, SHORT_HIST, ch), F32),
                   jax.ShapeDtypeStruct(s0.shape, F32)],
        scratch_shapes=[pltpu.VMEM((bt, SHORT_PAD + tt, ch), F32), pltpu.VMEM((bt * tt, d), BF16),
                        heads(QK_HEADS), heads(QK_HEADS), heads(V_HEADS), heads(V_HEADS),
                        heads(V_HEADS), heads(V_HEADS), pltpu.VMEM((bt * V_HEADS, 1, tt), F32),
                        heads(V_HEADS), pltpu.VMEM((bt * V_HEADS, tt, HEAD_DIM), BF16),
                        pltpu.VMEM((bt * V_HEADS, tt, HEAD_DIM), BF16),
                        pltpu.VMEM((bt * V_HEADS // PACK * (tt // CHUNK), PACK * CHUNK, PACK * CHUNK), BF16),
                        pltpu.VMEM((bt * V_HEADS * (tt // CHUNK), HEAD_DIM, CHUNK), BF16),
                        pltpu.VMEM((bt * V_HEADS * (tt // CHUNK), 1, HEAD_DIM), F32)],
        compiler_params=_params(2),
        name="gdn_mixer",
    )(x, mod, hist, s0, p["g_pre"], p["w_qkv"], p["w_z"], p["w_ab"], p["w_conv"], p["a_log"], p["dt_bias"],
      p["g_norm"], p["w_out"], p["g_post"])


def _row(v):
    return v.reshape(1, -1)


def _pad_lanes(v):
    return jnp.pad(v.reshape(1, -1), ((0, 0), (0, LANES - v.shape[-1])))


def _trunk(x, mod, conv_hist, s0, gdn_hist, w, *, conv_tile, ffn_tile, gdn_tile):
    bsz = x.shape[0]
    x, conv_cache = _conv_mixer(x, mod[0], conv_hist, w["conv"], bt=conv_tile[0], tt=conv_tile[1])
    x = _ffn(x, mod[0], w["ffn"][0], bt=ffn_tile[0], tt=ffn_tile[1])
    x, gdn_cache, s = _gdn_mixer(x, mod[1], gdn_hist, s0.reshape(bsz * V_HEADS, HEAD_DIM, HEAD_DIM), w["gdn"],
                                 bt=gdn_tile[0], tt=gdn_tile[1])
    x = _ffn(x, mod[1], w["ffn"][1], bt=ffn_tile[0], tt=ffn_tile[1])
    return x, conv_cache[None], s.reshape((1,) + s0.shape), gdn_cache[None]


def kernel(x_prompt, x_sample, c_prompt, c_sample, cache_conv, state_gdn, cache_gdn_conv, w_ada, b_ada, g_pre_mix, g_post_mix, g_pre_ffn, g_post_ffn, w_ffn_gate, w_ffn_up, w_ffn_down, w_pw1, b_pw1, w_dw, b_dw, ln_conv_g, ln_conv_b, w_pw2, b_pw2, w_gdn_in, w_gdn_conv, gdn_A_log, gdn_dt_bias, g_gdn_out_norm, w_gdn_out):
    n_prompt = x_prompt.shape[0]
    d = x_prompt.shape[-1]
    qkv_width = 2 * QK_HEADS * HEAD_DIM + V_HEADS * HEAD_DIM
    v_width = V_HEADS * HEAD_DIM

    mod = _modulation(jnp.concatenate([c_prompt, c_sample], axis=0), w_ada, b_ada)[:, :, None, :]
    mod_prompt, mod_sample = mod[:, :n_prompt], mod[:, n_prompt:]

    w_in = w_gdn_in[0]
    w_a = w_in[:, qkv_width + v_width:qkv_width + v_width + V_HEADS]
    w_b = w_in[:, qkv_width + v_width + V_HEADS:]
    lane_pad = ((0, 0), (0, LANES - V_HEADS))
    w = {
        "conv": dict(g_pre=_row(g_pre_mix[0]), w_pw1=w_pw1[0].astype(BF16), b_pw1=_row(b_pw1[0]), w_dw=w_dw[0],
                     b_dw=_row(b_dw[0]), ln_g=_row(ln_conv_g[0]), ln_b=_row(ln_conv_b[0]),
                     w_pw2=w_pw2[0].astype(BF16), b_pw2=_row(b_pw2[0]), g_post=_row(g_post_mix[0])),
        "ffn": [dict(g_pre=_row(g_pre_ffn[i]), w_gate=w_ffn_gate[i].astype(BF16), w_up=w_ffn_up[i].astype(BF16),
                     w_down=w_ffn_down[i].astype(BF16), g_post=_row(g_post_ffn[i])) for i in range(2)],
        "gdn": dict(g_pre=_row(g_pre_mix[1]), w_qkv=w_in[:, :qkv_width].astype(BF16),
                    w_z=w_in[:, qkv_width:qkv_width + v_width].astype(BF16),
                    w_ab=jnp.concatenate([jnp.pad(w_a, lane_pad), jnp.pad(w_b, lane_pad)], axis=1).astype(BF16),
                    w_conv=w_gdn_conv[0], a_log=_pad_lanes(gdn_A_log[0]), dt_bias=_pad_lanes(gdn_dt_bias[0]),
                    g_norm=_row(g_gdn_out_norm[0]), w_out=w_gdn_out[0].astype(BF16), g_post=_row(g_post_mix[1])),
    }

    zeros = lambda shape: jnp.zeros(shape, F32)
    y_prompt, conv_p, s_p, gconv_p = _trunk(
        x_prompt, mod_prompt, zeros((n_prompt,) + cache_conv.shape[2:]), zeros((n_prompt,) + state_gdn.shape[2:]),
        zeros((n_prompt,) + cache_gdn_conv.shape[2:]), w,
        conv_tile=(1, 256), ffn_tile=(1, 512), gdn_tile=(1, 256))
    y_sample, conv_s, s_s, gconv_s = _trunk(
        x_sample, mod_sample, cache_conv[0], state_gdn[0], cache_gdn_conv[0], w,
        conv_tile=(4, 64), ffn_tile=(8, 64), gdn_tile=(2, 64))
    return (y_prompt, y_sample, conv_p, conv_s, s_p, s_s, gconv_p, gconv_s)
```

```python
import functools

import jax
import jax.numpy as jnp
from jax import lax
from jax.experimental import pallas as pl
from jax.experimental.pallas import tpu as pltpu

F32 = jnp.float32
BF16 = jnp.bfloat16
EPS = 1e-6

SUBLANES = 8
LANES = 128
V7X_VMEM_BYTES = 64 * 1024 * 1024
VMEM_LIMIT = V7X_VMEM_BYTES * 7 // 8

CONV_WIDTH = 31
CONV_HIST = CONV_WIDTH - 1
CONV_PAD = 32
SHORT_WIDTH = 4
SHORT_HIST = SHORT_WIDTH - 1
SHORT_PAD = SUBLANES
HEAD_DIM = 128
QK_HEADS = 8
V_HEADS = 16
CHUNK = 64
TAP_STRIDE = 4
PACK = 2


def _rms(x, g):
    return x * lax.rsqrt(jnp.mean(x * x, axis=-1, keepdims=True) + EPS) * g


def _silu(x):
    return x * jax.nn.sigmoid(x)


def _softplus(x):
    return jnp.maximum(x, 0.0) + jnp.log1p(jnp.exp(-jnp.abs(x)))


def _mm(a, b):
    return jnp.dot(a, b, preferred_element_type=F32)


def _mm_nt(a, b):
    return lax.dot_general(a, b, (((1,), (1,)), ((), ())), preferred_element_type=F32)


def _const_spec(shape):
    zeros = (0,) * len(shape)
    return pl.BlockSpec(shape, lambda *_: zeros, pipeline_mode=pl.Buffered(1))


def _params(n_axes):
    return pltpu.CompilerParams(dimension_semantics=("arbitrary",) * n_axes, vmem_limit_bytes=VMEM_LIMIT)


def _modulation_kernel(c_ref, w_ref, b_ref, o_ref):
    c = c_ref[...]
    o_ref[...] = _mm(_silu(c).astype(BF16), w_ref[...].astype(BF16)) + b_ref[...]


def _modulation(c, w_ada, b_ada):
    depth, d, n = w_ada.shape
    rows = c.shape[0]
    bn = n // 4
    return pl.pallas_call(
        _modulation_kernel,
        grid=(depth, n // bn),
        in_specs=[
            pl.BlockSpec((rows, d), lambda i, j: (0, 0)),
            pl.BlockSpec((None, d, bn), lambda i, j: (i, 0, j)),
            pl.BlockSpec((None, 1, bn), lambda i, j: (i, 0, j)),
        ],
        out_specs=pl.BlockSpec((None, rows, bn), lambda i, j: (i, 0, j)),
        out_shape=jax.ShapeDtypeStruct((depth, rows, n), F32),
        compiler_params=_params(2),
        name="modulation",
    )(c, w_ada, b_ada.reshape(depth, 1, n))


def _conv_mixer_kernel(x_ref, mod_ref, hist_ref, gpre_ref, wpw1_ref, bpw1_ref, wdw_ref, bdw_ref,
                       lng_ref, lnb_ref, wpw2_ref, bpw2_ref, gpost_ref,
                       out_ref, cache_ref, ext_ref, y_ref, wb_ref):
    bt, tt, d = x_ref.shape
    slabs = d // LANES
    lanes_of = lambda sl: slice(sl * LANES, (sl + 1) * LANES)
    hist_rows = slice(CONV_PAD - CONV_HIST, CONV_PAD)

    @pl.when((pl.program_id(0) == 0) & (pl.program_id(1) == 0))
    def _():
        for k in range(CONV_WIDTH):
            wb_ref[k] = jnp.broadcast_to(wdw_ref[k:k + 1, :], (SUBLANES, d))

    @pl.when(pl.program_id(1) == 0)
    def _():
        for sl in range(slabs):
            ext_ref[sl, :, hist_rows, :] = hist_ref[:, :, lanes_of(sl)]

    x = x_ref[...]
    mod = mod_ref[...]
    shift, scale, gate = mod[:, :, :d], mod[:, :, d:2 * d], mod[:, :, 2 * d:]
    h = _rms(x, gpre_ref[...]) * (1.0 + scale) + shift
    a = _mm(h.reshape(bt * tt, d).astype(BF16), wpw1_ref[...]) + bpw1_ref[...]
    u = (a[:, :d] * jax.nn.sigmoid(a[:, d:])).reshape(bt, tt, d)
    for sl in range(slabs):
        ext_ref[sl, :, CONV_PAD:CONV_PAD + tt, :] = u[:, :, lanes_of(sl)]

    group = SUBLANES * TAP_STRIDE
    for sl in range(slabs):
        w_taps = [wb_ref[k, :, lanes_of(sl)] for k in range(CONV_WIDTH)]
        bias = jnp.broadcast_to(bdw_ref[:, lanes_of(sl)], (SUBLANES, LANES))
        for g0 in range(0, tt, group):
            for b in range(bt):
                accs = [bias] * TAP_STRIDE
                for k in range(CONV_WIDTH):
                    for r in range(TAP_STRIDE):
                        rows = pl.ds(g0 + (CONV_PAD - CONV_HIST + k + r), SUBLANES, stride=TAP_STRIDE)
                        accs[r] = accs[r] + ext_ref[sl, b, rows, :] * w_taps[k]
                for r in range(TAP_STRIDE):
                    y_ref[sl, b, pl.ds(g0 + r, SUBLANES, stride=TAP_STRIDE), :] = accs[r]

    y = jnp.concatenate([y_ref[sl] for sl in range(slabs)], axis=-1).reshape(bt * tt, d)
    mu = jnp.mean(y, axis=-1, keepdims=True)
    yc = y - mu
    var = jnp.mean(yc * yc, axis=-1, keepdims=True)
    yn = yc * lax.rsqrt(var + EPS) * lng_ref[...] + lnb_ref[...]
    m = _mm(_silu(yn).astype(BF16), wpw2_ref[...]) + bpw2_ref[...]
    out_ref[...] = x + gate * _rms(m, gpost_ref[...]).reshape(bt, tt, d)

    for sl in range(slabs):
        tail = ext_ref[sl, :, CONV_PAD + tt - CONV_HIST:CONV_PAD + tt, :]
        cache_ref[:, :, lanes_of(sl)] = tail
        ext_ref[sl, :, hist_rows, :] = tail


def _conv_mixer(x, mod, hist, p, *, bt, tt):
    bsz, seq, d = x.shape
    assert bsz % bt == 0 and seq % tt == 0 and tt % (SUBLANES * TAP_STRIDE) == 0 and tt >= CONV_HIST
    tok = pl.BlockSpec((bt, tt, d), lambda b, t: (b, t, 0))
    per_batch = lambda rows, cols: pl.BlockSpec((bt, rows, cols), lambda b, t: (b, 0, 0))
    return pl.pallas_call(
        _conv_mixer_kernel,
        grid=(bsz // bt, seq // tt),
        in_specs=[tok, per_batch(1, 3 * d), per_batch(CONV_HIST, d),
                  _const_spec((1, d)), _const_spec((d, 2 * d)), _const_spec((1, 2 * d)),
                  _const_spec((CONV_WIDTH, d)), _const_spec((1, d)), _const_spec((1, d)), _const_spec((1, d)),
                  _const_spec((d, d)), _const_spec((1, d)), _const_spec((1, d))],
        out_specs=[tok, per_batch(CONV_HIST, d)],
        out_shape=[jax.ShapeDtypeStruct(x.shape, F32), jax.ShapeDtypeStruct((bsz, CONV_HIST, d), F32)],
        scratch_shapes=[pltpu.VMEM((d // LANES, bt, CONV_PAD + tt, LANES), F32),
                        pltpu.VMEM((d // LANES, bt, tt, LANES), F32), pltpu.VMEM((CONV_WIDTH, SUBLANES, d), F32)],
        compiler_params=_params(2),
        name="conv_mixer",
    )(x, mod, hist, p["g_pre"], p["w_pw1"], p["b_pw1"], p["w_dw"], p["b_dw"], p["ln_g"], p["ln_b"],
      p["w_pw2"], p["b_pw2"], p["g_post"])


def _ffn_kernel(x_ref, mod_ref, gpre_ref, wg_ref, wu_ref, wd_ref, gpost_ref, out_ref):
    bt, tt, d = x_ref.shape
    x = x_ref[...]
    mod = mod_ref[...]
    shift, scale, gate = mod[:, :, :d], mod[:, :, d:2 * d], mod[:, :, 2 * d:]
    h = (_rms(x, gpre_ref[...]) * (1.0 + scale) + shift).reshape(bt * tt, d).astype(BF16)
    hid = _silu(_mm(h, wg_ref[...])) * _mm(h, wu_ref[...])
    f = _mm(hid.astype(BF16), wd_ref[...])
    out_ref[...] = x + gate * _rms(f, gpost_ref[...]).reshape(bt, tt, d)


def _ffn(x, mod, p, *, bt, tt):
    bsz, seq, d = x.shape
    f = p["w_gate"].shape[1]
    assert bsz % bt == 0 and seq % tt == 0
    tok = pl.BlockSpec((bt, tt, d), lambda b, t: (b, t, 0))
    return pl.pallas_call(
        _ffn_kernel,
        grid=(bsz // bt, seq // tt),
        in_specs=[tok, pl.BlockSpec((bt, 1, 3 * d), lambda b, t: (b, 0, 1)),
                  _const_spec((1, d)), _const_spec((d, f)), _const_spec((d, f)), _const_spec((f, d)),
                  _const_spec((1, d))],
        out_specs=tok,
        out_shape=jax.ShapeDtypeStruct(x.shape, F32),
        compiler_params=_params(2),
        name="ffn",
    )(x, mod, p["g_pre"], p["w_gate"], p["w_up"], p["w_down"], p["g_post"])


def _packed_unit_lower_inverses(ms, bd_mask):
    n, width = ms[0].shape
    r = lax.broadcasted_iota(jnp.int32, (n, width), 0)
    c = lax.broadcasted_iota(jnp.int32, (n, width), 1) & (n - 1)
    block_diag = lambda y: jnp.concatenate([y] * (width // n), axis=0) * bd_mask
    xs = [(r == c).astype(F32) - jnp.where((r >> 1) == (c >> 1), m, 0.0) for m in ms]
    level = 1
    while (1 << level) < n:
        couple = ((r >> (level + 1)) == (c >> (level + 1))) & ((r >> level) != (c >> level))
        xbs = [x.astype(BF16) for x in xs]
        xcs = [_mm(xb, block_diag(jnp.where(couple, m, 0.0).astype(BF16))).astype(BF16) for xb, m in zip(xbs, ms)]
        xs = [x - _mm(xc, block_diag(xb)) for x, xc, xb in zip(xs, xcs, xbs)]
        level += 1
    return xs


def _gdn_kernel(x_ref, mod_ref, hist_ref, s0_ref, gpre_ref, wqkv_ref, wz_ref, wab_ref, wconv_ref,
                alog_ref, dtb_ref, gnorm_ref, wout_ref, gpost_ref,
                out_ref, cache_ref, s_ref,
                ext_ref, hb_ref, q_ref, k_ref, v_ref, o_ref, gcb_ref, bb_ref, gr_ref,
                u_ref, w_ref, qd_ref, qkd_ref, kdt_ref, gt_ref):
    bt, tt, d = x_ref.shape
    rows = bt * tt
    n_chunks = tt // CHUNK
    slabs = wqkv_ref.shape[1] // LANES
    lanes_of = lambda sl: slice(sl * LANES, (sl + 1) * LANES)
    hist_rows = slice(SHORT_PAD - SHORT_HIST, SHORT_PAD)

    @pl.when(pl.program_id(1) == 0)
    def _():
        for sl in range(slabs):
            ext_ref[sl, :, hist_rows, :] = hist_ref[:, :, lanes_of(sl)]
        s_ref[...] = s0_ref[...]

    x = x_ref[...]
    mod = mod_ref[...]
    shift, scale, gate = mod[:, :, :d], mod[:, :, d:2 * d], mod[:, :, 2 * d:]
    hb = (_rms(x, gpre_ref[...]) * (1.0 + scale) + shift).reshape(rows, d).astype(BF16)
    hb_ref[...] = hb

    pre = _mm(hb, wqkv_ref[...]).reshape(bt, tt, slabs * LANES)
    for sl in range(slabs):
        ext_ref[sl, :, SHORT_PAD:SHORT_PAD + tt, :] = pre[:, :, lanes_of(sl)]
    group = SUBLANES * TAP_STRIDE
    for sl in range(slabs):
        w_taps = [jnp.broadcast_to(wconv_ref[k:k + 1, lanes_of(sl)], (SUBLANES, LANES)) for k in range(SHORT_WIDTH)]
        for b in range(bt):
            for t0 in range(0, tt, group):
                for r in range(TAP_STRIDE):
                    acc = None
                    for k in range(SHORT_WIDTH):
                        taps = pl.ds(t0 + r + (SHORT_PAD - SHORT_HIST + k), SUBLANES, stride=TAP_STRIDE)
                        term = ext_ref[sl, b, taps, :] * w_taps[k]
                        acc = term if acc is None else acc + term
                    val = _silu(acc)
                    out_rows = pl.ds(t0 + r, SUBLANES, stride=TAP_STRIDE)
                    if sl < 2 * QK_HEADS:
                        val = val * lax.rsqrt(jnp.sum(val * val, axis=-1, keepdims=True) + EPS)
                    if sl < QK_HEADS:
                        q_ref[b * QK_HEADS + sl, out_rows, :] = val * (HEAD_DIM ** -0.5)
                    elif sl < 2 * QK_HEADS:
                        k_ref[b * QK_HEADS + sl - QK_HEADS, out_rows, :] = val
                    else:
                        v_ref[b * V_HEADS + sl - 2 * QK_HEADS, out_rows, :] = val
    for sl in range(slabs):
        tail = ext_ref[sl, :, SHORT_PAD + tt - SHORT_HIST:SHORT_PAD + tt, :]
        cache_ref[:, :, lanes_of(sl)] = tail
        ext_ref[sl, :, hist_rows, :] = tail

    ab = _mm(hb, wab_ref[...])
    g = -jnp.exp(alog_ref[...]) * _softplus(ab[:, :LANES] + dtb_ref[...])
    beta = jax.nn.sigmoid(ab[:, LANES:])
    r_i = lax.broadcasted_iota(jnp.int32, (rows, rows), 0)
    c_i = lax.broadcasted_iota(jnp.int32, (rows, rows), 1)
    tri = ((r_i >= c_i) & (r_i // CHUNK == c_i // CHUNK)).astype(F32)
    gc = jnp.dot(tri, g, precision=lax.Precision.HIGHEST, preferred_element_type=F32)
    pad = (-rows) % LANES
    gc_t = (jnp.concatenate([gc, jnp.zeros((pad, LANES), F32)], axis=0) if pad else gc).T
    for b in range(bt):
        for hh in range(V_HEADS):
            n = b * V_HEADS + hh
            gcb_ref[n] = jnp.broadcast_to(gc[b * tt:(b + 1) * tt, hh:hh + 1], (tt, LANES))
            bb_ref[n] = jnp.broadcast_to(beta[b * tt:(b + 1) * tt, hh:hh + 1], (tt, LANES))
            gr_ref[n] = gc_t[hh:hh + 1, b * tt:(b + 1) * tt]

    width = PACK * CHUNK
    r_c = lax.broadcasted_iota(jnp.int32, (CHUNK, width), 0)
    c_c = lax.broadcasted_iota(jnp.int32, (CHUNK, width), 1) & (CHUNK - 1)
    incl = r_c >= c_c
    strict = r_c > c_c
    low_half = lax.broadcasted_iota(jnp.int32, (CHUNK, HEAD_DIM), 1) < CHUNK
    bd_mask = ((lax.broadcasted_iota(jnp.int32, (width, width), 0) // CHUNK)
               == (lax.broadcasted_iota(jnp.int32, (width, width), 1) // CHUNK)).astype(BF16)
    block_diag = lambda y: jnp.concatenate([y] * PACK, axis=0) * bd_mask
    zeros_half = jnp.zeros((CHUNK, HEAD_DIM), F32)
    rows_of = lambda c: slice(c * CHUNK, (c + 1) * CHUNK)

    packs = [(g, c) for g in range(bt * V_HEADS // PACK) for c in range(n_chunks)]
    pairs = [(g * (PACK // 2) + j, c) for g, c in packs for j in range(PACK // 2)]
    heads = [(i, 2 * n + hh, c) for i, (n, c) in enumerate(pairs) for hh in range(2)]
    qs = [q_ref[n, rows_of(c), :] for n, c in pairs]
    ks = [k_ref[n, rows_of(c), :] for n, c in pairs]
    kbs = [k.astype(BF16) for k in ks]
    prods = [_mm_nt(jnp.concatenate([q.astype(BF16), kb], axis=0), jnp.concatenate([kb, kb], axis=0))
             for q, kb in zip(qs, kbs)]
    k_ts = [jnp.concatenate([k, zeros_half], axis=0).T[:, :CHUNK] for k in ks]
    gcbs = [gcb_ref[vh, rows_of(c), :] for _, vh, c in heads]
    bbs = [bb_ref[vh, rows_of(c), :] for _, vh, c in heads]
    g_rows = [gr_ref[vh][:, rows_of(c)] for _, vh, c in heads]
    side_by_side = lambda per_pair, p: jnp.concatenate(per_pair[p * (PACK // 2):(p + 1) * (PACK // 2)], axis=1)
    pair_lanes = lambda vals: [jnp.where(low_half, vals[2 * i], vals[2 * i + 1]) for i in range(len(pairs))]
    gcb_pairs, bb_pairs = pair_lanes(gcbs), pair_lanes(bbs)
    g_row_pairs = [jnp.concatenate([g_rows[2 * i], g_rows[2 * i + 1]], axis=1) for i in range(len(pairs))]
    decays, m_packs = [], []
    for p in range(len(packs)):
        diff = side_by_side(gcb_pairs, p) - side_by_side(g_row_pairs, p)
        decay = jnp.where(incl, jnp.exp(jnp.where(incl, diff, 0.0)), 0.0)
        a_kk = side_by_side([pr[CHUNK:] for pr in prods], p)
        decays.append(decay)
        m_packs.append(jnp.where(strict, a_kk * side_by_side(bb_pairs, p) * decay, 0.0))
    t_invs = _packed_unit_lower_inverses(m_packs, bd_mask)
    egs = [jnp.exp(gcb) for gcb in gcbs]
    rhss = [jnp.concatenate([v_ref[vh, rows_of(c), :] * bb, ks[i] * (bb * eg)], axis=1).astype(BF16)
            for (i, vh, c), bb, eg in zip(heads, bbs, egs)]
    sols = [_mm(block_diag(t_inv.astype(BF16)), jnp.concatenate(rhss[p * PACK:(p + 1) * PACK], axis=0))
            for p, t_inv in enumerate(t_invs)]
    for p, (g, c) in enumerate(packs):
        a_qk = side_by_side([pr[:CHUNK] for pr in prods], p)
        qkd_ref[g * n_chunks + c] = block_diag((a_qk * decays[p]).astype(BF16))
    for h, ((i, vh, c), eg, gcb, g_row) in enumerate(zip(heads, egs, gcbs, g_rows)):
        rs = rows_of(c)
        sol = sols[h // PACK][(h % PACK) * CHUNK:(h % PACK + 1) * CHUNK]
        g_last = gcb[CHUNK - 1:CHUNK, :]
        u_ref[vh, rs, :] = sol[:, :HEAD_DIM]
        w_ref[vh, rs, :] = sol[:, HEAD_DIM:].astype(BF16)
        qd_ref[vh, rs, :] = (qs[i] * eg).astype(BF16)
        kdt_ref[vh * n_chunks + c] = (k_ts[i] * jnp.exp(g_last[:, :CHUNK] - g_row)).astype(BF16)
        gt_ref[vh * n_chunks + c] = jnp.exp(g_last)

    all_heads = range(bt * V_HEADS)
    for c in range(n_chunks):
        rs = rows_of(c)
        ss = [s_ref[vh] for vh in all_heads]
        ws_qs = [_mm(jnp.concatenate([w_ref[vh, rs, :], qd_ref[vh, rs, :]], axis=0), ss[vh].astype(BF16))
                 for vh in all_heads]
        v_news = [(u_ref[vh, rs, :] - ws_qs[vh][:CHUNK]).astype(BF16) for vh in all_heads]
        for g in range(bt * V_HEADS // PACK):
            qk_v = _mm(qkd_ref[g * n_chunks + c], jnp.concatenate(v_news[g * PACK:(g + 1) * PACK], axis=0))
            for j in range(PACK):
                vh = g * PACK + j
                o_ref[vh, rs, :] = ws_qs[vh][CHUNK:] + qk_v[j * CHUNK:(j + 1) * CHUNK]
        for vh in all_heads:
            s_ref[vh] = ss[vh] * gt_ref[vh * n_chunks + c] + _mm(kdt_ref[vh * n_chunks + c], v_news[vh])

    z = _mm(hb_ref[...], wz_ref[...])
    gated = []
    for hh in range(V_HEADS):
        o_h = jnp.concatenate([o_ref[b * V_HEADS + hh] for b in range(bt)], axis=0)
        gated.append((_rms(o_h, gnorm_ref[...]) * _silu(z[:, hh * HEAD_DIM:(hh + 1) * HEAD_DIM])).astype(BF16))
    m = _mm(jnp.concatenate(gated, axis=1), wout_ref[...])
    out_ref[...] = x + gate * _rms(m, gpost_ref[...]).reshape(bt, tt, d)


def _gdn_mixer(x, mod, hist, s0, p, *, bt, tt):
    bsz, seq, d = x.shape
    ch = hist.shape[-1]
    v_width = V_HEADS * HEAD_DIM
    assert bsz % bt == 0 and seq % tt == 0 and tt % CHUNK == 0 and HEAD_DIM == LANES
    tok = pl.BlockSpec((bt, tt, d), lambda b, t: (b, t, 0))
    per_batch = lambda rows, cols: pl.BlockSpec((bt, rows, cols), lambda b, t: (b, 0, 0))
    state = pl.BlockSpec((bt * V_HEADS, HEAD_DIM, HEAD_DIM), lambda b, t: (b, 0, 0))
    heads = lambda n: pltpu.VMEM((bt * n, tt, HEAD_DIM), F32)
    return pl.pallas_call(
        _gdn_kernel,
        grid=(bsz // bt, seq // tt),
        in_specs=[tok, per_batch(1, 3 * d), per_batch(SHORT_HIST, ch), state,
                  _const_spec((1, d)), _const_spec((d, ch)), _const_spec((d, v_width)), _const_spec((d, 2 * LANES)),
                  _const_spec((SHORT_WIDTH, ch)), _const_spec((1, LANES)), _const_spec((1, LANES)),
                  _const_spec((1, HEAD_DIM)), _const_spec((v_width, d)), _const_spec((1, d))],
        out_specs=[tok, per_batch(SHORT_HIST, ch), state],
        out_shape=[jax.ShapeDtypeStruct(x.shape, F32), jax.ShapeDtypeStruct((bsz, SHORT_HIST, ch), F32),
                   jax.ShapeDtypeStruct(s0.shape, F32)],
        scratch_shapes=[pltpu.VMEM((ch // LANES, bt, SHORT_PAD + tt, LANES), F32), pltpu.VMEM((bt * tt, d), BF16),
                        heads(QK_HEADS), heads(QK_HEADS), heads(V_HEADS), heads(V_HEADS),
                        heads(V_HEADS), heads(V_HEADS), pltpu.VMEM((bt * V_HEADS, 1, tt), F32),
                        heads(V_HEADS), pltpu.VMEM((bt * V_HEADS, tt, HEAD_DIM), BF16),
                        pltpu.VMEM((bt * V_HEADS, tt, HEAD_DIM), BF16),
                        pltpu.VMEM((bt * V_HEADS // PACK * (tt // CHUNK), PACK * CHUNK, PACK * CHUNK), BF16),
                        pltpu.VMEM((bt * V_HEADS * (tt // CHUNK), HEAD_DIM, CHUNK), BF16),
                        pltpu.VMEM((bt * V_HEADS * (tt // CHUNK), 1, HEAD_DIM), F32)],
        compiler_params=_params(2),
        name="gdn_mixer",
    )(x, mod, hist, s0, p["g_pre"], p["w_qkv"], p["w_z"], p["w_ab"], p["w_conv"], p["a_log"], p["dt_bias"],
      p["g_norm"], p["w_out"], p["g_post"])


def _row(v):
    return v.reshape(1, -1)


def _pad_lanes(v):
    return jnp.pad(v.reshape(1, -1), ((0, 0), (0, LANES - v.shape[-1])))


def _trunk(x, mod, conv_hist, s0, gdn_hist, w, *, conv_tile, ffn_tile, gdn_tile):
    bsz = x.shape[0]
    x, conv_cache = _conv_mixer(x, mod[0], conv_hist, w["conv"], bt=conv_tile[0], tt=conv_tile[1])
    x = _ffn(x, mod[0], w["ffn"][0], bt=ffn_tile[0], tt=ffn_tile[1])
    x, gdn_cache, s = _gdn_mixer(x, mod[1], gdn_hist, s0.reshape(bsz * V_HEADS, HEAD_DIM, HEAD_DIM), w["gdn"],
                                 bt=gdn_tile[0], tt=gdn_tile[1])
    x = _ffn(x, mod[1], w["ffn"][1], bt=ffn_tile[0], tt=ffn_tile[1])
    return x, conv_cache[None], s.reshape((1,) + s0.shape), gdn_cache[None]


def kernel(x_prompt, x_sample, c_prompt, c_sample, cache_conv, state_gdn, cache_gdn_conv, w_ada, b_ada, g_pre_mix, g_post_mix, g_pre_ffn, g_post_ffn, w_ffn_gate, w_ffn_up, w_ffn_down, w_pw1, b_pw1, w_dw, b_dw, ln_conv_g, ln_conv_b, w_pw2, b_pw2, w_gdn_in, w_gdn_conv, gdn_A_log, gdn_dt_bias, g_gdn_out_norm, w_gdn_out):
    n_prompt = x_prompt.shape[0]
    d = x_prompt.shape[-1]
    qkv_width = 2 * QK_HEADS * HEAD_DIM + V_HEADS * HEAD_DIM
    v_width = V_HEADS * HEAD_DIM

    mod = _modulation(jnp.concatenate([c_prompt, c_sample], axis=0), w_ada, b_ada)[:, :, None, :]
    mod_prompt, mod_sample = mod[:, :n_prompt], mod[:, n_prompt:]

    w_in = w_gdn_in[0]
    w_a = w_in[:, qkv_width + v_width:qkv_width + v_width + V_HEADS]
    w_b = w_in[:, qkv_width + v_width + V_HEADS:]
    lane_pad = ((0, 0), (0, LANES - V_HEADS))
    w = {
        "conv": dict(g_pre=_row(g_pre_mix[0]), w_pw1=w_pw1[0].astype(BF16), b_pw1=_row(b_pw1[0]), w_dw=w_dw[0],
                     b_dw=_row(b_dw[0]), ln_g=_row(ln_conv_g[0]), ln_b=_row(ln_conv_b[0]),
                     w_pw2=w_pw2[0].astype(BF16), b_pw2=_row(b_pw2[0]), g_post=_row(g_post_mix[0])),
        "ffn": [dict(g_pre=_row(g_pre_ffn[i]), w_gate=w_ffn_gate[i].astype(BF16), w_up=w_ffn_up[i].astype(BF16),
                     w_down=w_ffn_down[i].astype(BF16), g_post=_row(g_post_ffn[i])) for i in range(2)],
        "gdn": dict(g_pre=_row(g_pre_mix[1]), w_qkv=w_in[:, :qkv_width].astype(BF16),
                    w_z=w_in[:, qkv_width:qkv_width + v_width].astype(BF16),
                    w_ab=jnp.concatenate([jnp.pad(w_a, lane_pad), jnp.pad(w_b, lane_pad)], axis=1).astype(BF16),
                    w_conv=w_gdn_conv[0], a_log=_pad_lanes(gdn_A_log[0]), dt_bias=_pad_lanes(gdn_dt_bias[0]),
                    g_norm=_row(g_gdn_out_norm[0]), w_out=w_gdn_out[0].astype(BF16), g_post=_row(g_post_mix[1])),
    }

    zeros = lambda shape: jnp.zeros(shape, F32)
    y_prompt, conv_p, s_p, gconv_p = _trunk(
        x_prompt, mod_prompt, zeros((n_prompt,) + cache_conv.shape[2:]), zeros((n_prompt,) + state_gdn.shape[2:]),
        zeros((n_prompt,) + cache_gdn_conv.shape[2:]), w,
        conv_tile=(1, 256), ffn_tile=(1, 512), gdn_tile=(1, 256))
    y_sample, conv_s, s_s, gconv_s = _trunk(
        x_sample, mod_sample, cache_conv[0], state_gdn[0], cache_gdn_conv[0], w,
        conv_tile=(4, 64), ffn_tile=(8, 64), gdn_tile=(2, 64))
    return (y_prompt, y_sample, conv_p, conv_s, s_p, s_s, gconv_p, gconv_s)
```

```python
import functools

import jax
import jax.numpy as jnp
from jax import lax
from jax.experimental import pallas as pl
from jax.experimental.pallas import tpu as pltpu

F32 = jnp.float32
BF16 = jnp.bfloat16
EPS = 1e-6

SUBLANES = 8
LANES = 128
V7X_VMEM_BYTES = 64 * 1024 * 1024
VMEM_LIMIT = V7X_VMEM_BYTES * 7 // 8

CONV_WIDTH = 31
CONV_HIST = CONV_WIDTH - 1
CONV_PAD = 32
SHORT_WIDTH = 4
SHORT_HIST = SHORT_WIDTH - 1
SHORT_PAD = SUBLANES
HEAD_DIM = 128
QK_HEADS = 8
V_HEADS = 16
CHUNK = 64
TAP_STRIDE = 4
FFN_SUBTILES = 2
PACK = 2


def _rms(x, g):
    return x * lax.rsqrt(jnp.mean(x * x, axis=-1, keepdims=True) + EPS) * g


def _silu(x):
    return x * jax.nn.sigmoid(x)


def _softplus(x):
    return jnp.maximum(x, 0.0) + jnp.log1p(jnp.exp(-jnp.abs(x)))


def _mm(a, b):
    return jnp.dot(a, b, preferred_element_type=F32)


def _mm_nt(a, b):
    return lax.dot_general(a, b, (((1,), (1,)), ((), ())), preferred_element_type=F32)


def _const_spec(shape):
    zeros = (0,) * len(shape)
    return pl.BlockSpec(shape, lambda *_: zeros, pipeline_mode=pl.Buffered(1))


def _params(n_axes):
    return pltpu.CompilerParams(dimension_semantics=("arbitrary",) * n_axes, vmem_limit_bytes=VMEM_LIMIT)


def _modulation_kernel(c_ref, w_ref, b_ref, o_ref):
    c = c_ref[...]
    o_ref[...] = _mm(_silu(c).astype(BF16), w_ref[...].astype(BF16)) + b_ref[...]


def _modulation(c, w_ada, b_ada):
    depth, d, n = w_ada.shape
    rows = c.shape[0]
    bn = n // 4
    return pl.pallas_call(
        _modulation_kernel,
        grid=(depth, n // bn),
        in_specs=[
            pl.BlockSpec((rows, d), lambda i, j: (0, 0)),
            pl.BlockSpec((None, d, bn), lambda i, j: (i, 0, j)),
            pl.BlockSpec((None, 1, bn), lambda i, j: (i, 0, j)),
        ],
        out_specs=pl.BlockSpec((None, rows, bn), lambda i, j: (i, 0, j)),
        out_shape=jax.ShapeDtypeStruct((depth, rows, n), F32),
        compiler_params=_params(2),
        name="modulation",
    )(c, w_ada, b_ada.reshape(depth, 1, n))


def _conv_mixer_kernel(x_ref, mod_ref, hist_ref, gpre_ref, wpw1_ref, bpw1_ref, wdw_ref, bdw_ref,
                       lng_ref, lnb_ref, wpw2_ref, bpw2_ref, gpost_ref,
                       out_ref, cache_ref, ext_ref, y_ref, wb_ref):
    bt, tt, d = x_ref.shape
    slabs = d // LANES
    lanes_of = lambda sl: slice(sl * LANES, (sl + 1) * LANES)
    hist_rows = slice(CONV_PAD - CONV_HIST, CONV_PAD)

    @pl.when((pl.program_id(0) == 0) & (pl.program_id(1) == 0))
    def _():
        for k in range(CONV_WIDTH):
            wb_ref[k] = jnp.broadcast_to(wdw_ref[k:k + 1, :], (SUBLANES, d))

    @pl.when(pl.program_id(1) == 0)
    def _():
        for sl in range(slabs):
            ext_ref[sl, :, hist_rows, :] = hist_ref[:, :, lanes_of(sl)]

    x = x_ref[...]
    mod = mod_ref[...]
    shift, scale, gate = mod[:, :, :d], mod[:, :, d:2 * d], mod[:, :, 2 * d:]
    h = _rms(x, gpre_ref[...]) * (1.0 + scale) + shift
    a = _mm(h.reshape(bt * tt, d).astype(BF16), wpw1_ref[...]) + bpw1_ref[...]
    u = (a[:, :d] * jax.nn.sigmoid(a[:, d:])).reshape(bt, tt, d)
    for sl in range(slabs):
        ext_ref[sl, :, CONV_PAD:CONV_PAD + tt, :] = u[:, :, lanes_of(sl)]

    group = SUBLANES * TAP_STRIDE
    for sl in range(slabs):
        w_taps = [wb_ref[k, :, lanes_of(sl)] for k in range(CONV_WIDTH)]
        bias = jnp.broadcast_to(bdw_ref[:, lanes_of(sl)], (SUBLANES, LANES))
        for g0 in range(0, tt, group):
            for b in range(bt):
                accs = [bias] * TAP_STRIDE
                for k in range(CONV_WIDTH):
                    for r in range(TAP_STRIDE):
                        rows = pl.ds(g0 + (CONV_PAD - CONV_HIST + k + r), SUBLANES, stride=TAP_STRIDE)
                        accs[r] = accs[r] + ext_ref[sl, b, rows, :] * w_taps[k]
                for r in range(TAP_STRIDE):
                    y_ref[sl, b, pl.ds(g0 + r, SUBLANES, stride=TAP_STRIDE), :] = accs[r]

    y = jnp.concatenate([y_ref[sl] for sl in range(slabs)], axis=-1).reshape(bt * tt, d)
    mu = jnp.mean(y, axis=-1, keepdims=True)
    yc = y - mu
    var = jnp.mean(yc * yc, axis=-1, keepdims=True)
    yn = yc * lax.rsqrt(var + EPS) * lng_ref[...] + lnb_ref[...]
    m = _mm(_silu(yn).astype(BF16), wpw2_ref[...]) + bpw2_ref[...]
    out_ref[...] = x + gate * _rms(m, gpost_ref[...]).reshape(bt, tt, d)

    for sl in range(slabs):
        tail = ext_ref[sl, :, CONV_PAD + tt - CONV_HIST:CONV_PAD + tt, :]
        cache_ref[:, :, lanes_of(sl)] = tail
        ext_ref[sl, :, hist_rows, :] = tail


def _conv_mixer(x, mod, hist, p, *, bt, tt):
    bsz, seq, d = x.shape
    assert bsz % bt == 0 and seq % tt == 0 and tt % (SUBLANES * TAP_STRIDE) == 0 and tt >= CONV_HIST
    tok = pl.BlockSpec((bt, tt, d), lambda b, t: (b, t, 0))
    per_batch = lambda rows, cols: pl.BlockSpec((bt, rows, cols), lambda b, t: (b, 0, 0))
    return pl.pallas_call(
        _conv_mixer_kernel,
        grid=(bsz // bt, seq // tt),
        in_specs=[tok, per_batch(1, 3 * d), per_batch(CONV_HIST, d),
                  _const_spec((1, d)), _const_spec((d, 2 * d)), _const_spec((1, 2 * d)),
                  _const_spec((CONV_WIDTH, d)), _const_spec((1, d)), _const_spec((1, d)), _const_spec((1, d)),
                  _const_spec((d, d)), _const_spec((1, d)), _const_spec((1, d))],
        out_specs=[tok, per_batch(CONV_HIST, d)],
        out_shape=[jax.ShapeDtypeStruct(x.shape, F32), jax.ShapeDtypeStruct((bsz, CONV_HIST, d), F32)],
        scratch_shapes=[pltpu.VMEM((d // LANES, bt, CONV_PAD + tt, LANES), F32),
                        pltpu.VMEM((d // LANES, bt, tt, LANES), F32), pltpu.VMEM((CONV_WIDTH, SUBLANES, d), F32)],
        compiler_params=_params(2),
        name="conv_mixer",
    )(x, mod, hist, p["g_pre"], p["w_pw1"], p["b_pw1"], p["w_dw"], p["b_dw"], p["ln_g"], p["ln_b"],
      p["w_pw2"], p["b_pw2"], p["g_post"])


def _ffn_kernel(x_ref, mod_ref, gpre_ref, wg_ref, wu_ref, wd_ref, gpost_ref, out_ref):
    bt, tt, d = x_ref.shape
    mod = mod_ref[...]
    if bt >= FFN_SUBTILES:
        parts = [(slice(i * bt // FFN_SUBTILES, (i + 1) * bt // FFN_SUBTILES), slice(None)) for i in range(FFN_SUBTILES)]
    else:
        parts = [(slice(None), slice(i * tt // FFN_SUBTILES, (i + 1) * tt // FFN_SUBTILES)) for i in range(FFN_SUBTILES)]
    for bs, ts in parts:
        x = x_ref[bs, ts, :]
        pb, pt, _ = x.shape
        m = mod[bs]
        shift, scale, gate = m[:, :, :d], m[:, :, d:2 * d], m[:, :, 2 * d:]
        h = (_rms(x, gpre_ref[...]) * (1.0 + scale) + shift).reshape(pb * pt, d).astype(BF16)
        hid = _silu(_mm(h, wg_ref[...])) * _mm(h, wu_ref[...])
        f = _mm(hid.astype(BF16), wd_ref[...])
        out_ref[bs, ts, :] = x + gate * _rms(f, gpost_ref[...]).reshape(pb, pt, d)


def _ffn(x, mod, p, *, bt, tt):
    bsz, seq, d = x.shape
    f = p["w_gate"].shape[1]
    assert bsz % bt == 0 and seq % tt == 0
    tok = pl.BlockSpec((bt, tt, d), lambda b, t: (b, t, 0))
    return pl.pallas_call(
        _ffn_kernel,
        grid=(bsz // bt, seq // tt),
        in_specs=[tok, pl.BlockSpec((bt, 1, 3 * d), lambda b, t: (b, 0, 1)),
                  _const_spec((1, d)), _const_spec((d, f)), _const_spec((d, f)), _const_spec((f, d)),
                  _const_spec((1, d))],
        out_specs=tok,
        out_shape=jax.ShapeDtypeStruct(x.shape, F32),
        compiler_params=_params(2),
        name="ffn",
    )(x, mod, p["g_pre"], p["w_gate"], p["w_up"], p["w_down"], p["g_post"])


def _packed_unit_lower_inverses(ms):
    n, width = ms[0].shape
    r = lax.broadcasted_iota(jnp.int32, (n, width), 0)
    c = lax.broadcasted_iota(jnp.int32, (n, width), 1) & (n - 1)
    big_r = lax.broadcasted_iota(jnp.int32, (width, width), 0)
    big_c = lax.broadcasted_iota(jnp.int32, (width, width), 1)
    same_block = (big_r // n) == (big_c // n)
    rr, cc = big_r & (n - 1), big_c & (n - 1)
    stack = lambda y: jnp.concatenate([y] * (width // n), axis=0)
    bd_mask = same_block.astype(BF16)
    mbs = [stack(m.astype(BF16)) for m in ms]
    xs = [(r == c).astype(F32) - jnp.where((r >> 1) == (c >> 1), m, 0.0) for m in ms]
    level = 1
    while (1 << level) < n:
        couple = same_block & ((rr >> (level + 1)) == (cc >> (level + 1))) & ((rr >> level) != (cc >> level))
        couple = couple.astype(BF16)
        xbs = [x.astype(BF16) for x in xs]
        xcs = [_mm(xb, mb * couple).astype(BF16) for xb, mb in zip(xbs, mbs)]
        xs = [x - _mm(xc, stack(xb) * bd_mask) for x, xc, xb in zip(xs, xcs, xbs)]
        level += 1
    return xs


def _gdn_kernel(x_ref, mod_ref, hist_ref, s0_ref, gpre_ref, wqkv_ref, wz_ref, wab_ref, wconv_ref,
                alog_ref, dtb_ref, gnorm_ref, wout_ref, gpost_ref,
                out_ref, cache_ref, s_ref,
                ext_ref, q_ref, k_ref, v_ref, o_ref, gcb_ref, bb_ref, gr_ref,
                u_ref, w_ref, qd_ref, qkd_ref, kdt_ref, gt_ref, z_ref):
    bt, tt, d = x_ref.shape
    rows = bt * tt
    n_chunks = tt // CHUNK
    slabs = wqkv_ref.shape[1] // LANES
    lanes_of = lambda sl: slice(sl * LANES, (sl + 1) * LANES)
    hist_rows = slice(SHORT_PAD - SHORT_HIST, SHORT_PAD)

    @pl.when(pl.program_id(1) == 0)
    def _():
        for sl in range(slabs):
            ext_ref[sl, :, hist_rows, :] = hist_ref[:, :, lanes_of(sl)]
        s_ref[...] = s0_ref[...]

    x = x_ref[...]
    mod = mod_ref[...]
    shift, scale, gate = mod[:, :, :d], mod[:, :, d:2 * d], mod[:, :, 2 * d:]
    hb = (_rms(x, gpre_ref[...]) * (1.0 + scale) + shift).reshape(rows, d).astype(BF16)

    ab = _mm(hb, wab_ref[...])
    g = -jnp.exp(alog_ref[...]) * _softplus(ab[:, :LANES] + dtb_ref[...])
    beta = jax.nn.sigmoid(ab[:, LANES:])
    r_i = lax.broadcasted_iota(jnp.int32, (rows, rows), 0)
    c_i = lax.broadcasted_iota(jnp.int32, (rows, rows), 1)
    tri = ((r_i >= c_i) & (r_i // CHUNK == c_i // CHUNK)).astype(F32)
    gc = jnp.dot(tri, g, precision=lax.Precision.HIGHEST, preferred_element_type=F32)
    pad = (-rows) % LANES
    gc_t = (jnp.concatenate([gc, jnp.zeros((pad, LANES), F32)], axis=0) if pad else gc).T
    for b in range(bt):
        for hh in range(V_HEADS):
            n = b * V_HEADS + hh
            gcb_ref[n] = jnp.broadcast_to(gc[b * tt:(b + 1) * tt, hh:hh + 1], (tt, LANES))
            bb_ref[n] = jnp.broadcast_to(beta[b * tt:(b + 1) * tt, hh:hh + 1], (tt, LANES))
            gr_ref[n] = gc_t[hh:hh + 1, b * tt:(b + 1) * tt]

    pre = _mm(hb, wqkv_ref[...]).reshape(bt, tt, slabs * LANES)
    z_ref[...] = _mm(hb, wz_ref[...])
    for sl in range(slabs):
        ext_ref[sl, :, SHORT_PAD:SHORT_PAD + tt, :] = pre[:, :, lanes_of(sl)]
    group = SUBLANES * TAP_STRIDE
    for sl in range(slabs):
        w_taps = [jnp.broadcast_to(wconv_ref[k:k + 1, lanes_of(sl)], (SUBLANES, LANES)) for k in range(SHORT_WIDTH)]
        for b in range(bt):
            for t0 in range(0, tt, group):
                for r in range(TAP_STRIDE):
                    acc = None
                    for k in range(SHORT_WIDTH):
                        taps = pl.ds(t0 + r + (SHORT_PAD - SHORT_HIST + k), SUBLANES, stride=TAP_STRIDE)
                        term = ext_ref[sl, b, taps, :] * w_taps[k]
                        acc = term if acc is None else acc + term
                    val = _silu(acc)
                    out_rows = pl.ds(t0 + r, SUBLANES, stride=TAP_STRIDE)
                    if sl < 2 * QK_HEADS:
                        val = val * lax.rsqrt(jnp.sum(val * val, axis=-1, keepdims=True) + EPS)
                    if sl < QK_HEADS:
                        q_ref[b * QK_HEADS + sl, out_rows, :] = val * (HEAD_DIM ** -0.5)
                    elif sl < 2 * QK_HEADS:
                        k_ref[b * QK_HEADS + sl - QK_HEADS, out_rows, :] = val
                    else:
                        v_ref[b * V_HEADS + sl - 2 * QK_HEADS, out_rows, :] = val
    for sl in range(slabs):
        tail = ext_ref[sl, :, SHORT_PAD + tt - SHORT_HIST:SHORT_PAD + tt, :]
        cache_ref[:, :, lanes_of(sl)] = tail
        ext_ref[sl, :, hist_rows, :] = tail

    width = PACK * CHUNK
    r_c = lax.broadcasted_iota(jnp.int32, (CHUNK, width), 0)
    c_c = lax.broadcasted_iota(jnp.int32, (CHUNK, width), 1) & (CHUNK - 1)
    incl = r_c >= c_c
    strict = r_c > c_c
    low_half = lax.broadcasted_iota(jnp.int32, (CHUNK, HEAD_DIM), 1) < CHUNK
    bd_mask = ((lax.broadcasted_iota(jnp.int32, (width, width), 0) // CHUNK)
               == (lax.broadcasted_iota(jnp.int32, (width, width), 1) // CHUNK)).astype(BF16)
    block_diag = lambda y: jnp.concatenate([y] * PACK, axis=0) * bd_mask
    rows_of = lambda c: slice(c * CHUNK, (c + 1) * CHUNK)

    packs = [(g, c) for g in range(bt * V_HEADS // PACK) for c in range(n_chunks)]
    pairs = [(g * (PACK // 2) + j, c) for g, c in packs for j in range(PACK // 2)]
    heads = [(i, 2 * n + hh, c) for i, (n, c) in enumerate(pairs) for hh in range(2)]
    qs = [q_ref[n, rows_of(c), :] for n, c in pairs]
    ks = [k_ref[n, rows_of(c), :] for n, c in pairs]
    kbs = [k.astype(BF16) for k in ks]
    prods = [_mm_nt(jnp.concatenate([q.astype(BF16), kb], axis=0), jnp.concatenate([kb, kb], axis=0))
             for q, kb in zip(qs, kbs)]
    k_ts = [jnp.concatenate([k, k], axis=0).T for k in ks]
    gcbs = [gcb_ref[vh, rows_of(c), :] for _, vh, c in heads]
    bbs = [bb_ref[vh, rows_of(c), :] for _, vh, c in heads]
    g_rows = [gr_ref[vh][:, rows_of(c)] for _, vh, c in heads]
    side_by_side = lambda per_pair, p: jnp.concatenate(per_pair[p * (PACK // 2):(p + 1) * (PACK // 2)], axis=1)
    pair_lanes = lambda vals: [jnp.where(low_half, vals[2 * i], vals[2 * i + 1]) for i in range(len(pairs))]
    gcb_pairs, bb_pairs = pair_lanes(gcbs), pair_lanes(bbs)
    g_row_pairs = [jnp.concatenate([g_rows[2 * i], g_rows[2 * i + 1]], axis=1) for i in range(len(pairs))]
    decays, m_packs = [], []
    for p in range(len(packs)):
        diff = side_by_side(gcb_pairs, p) - side_by_side(g_row_pairs, p)
        decay = jnp.where(incl, jnp.exp(jnp.where(incl, diff, 0.0)), 0.0)
        a_kk = side_by_side([pr[CHUNK:] for pr in prods], p)
        decays.append(decay)
        m_packs.append(jnp.where(strict, a_kk * side_by_side(bb_pairs, p) * decay, 0.0))
    t_invs = _packed_unit_lower_inverses(m_packs)
    egs = [jnp.exp(gcb) for gcb in gcbs]
    rhss = [jnp.concatenate([v_ref[vh, rows_of(c), :] * bb, ks[i] * (bb * eg)], axis=1).astype(BF16)
            for (i, vh, c), bb, eg in zip(heads, bbs, egs)]
    sols = [_mm(block_diag(t_inv.astype(BF16)), jnp.concatenate(rhss[p * PACK:(p + 1) * PACK], axis=0))
            for p, t_inv in enumerate(t_invs)]
    for p, (g, c) in enumerate(packs):
        a_qk = side_by_side([pr[:CHUNK] for pr in prods], p)
        qkd_ref[g * n_chunks + c] = block_diag((a_qk * decays[p]).astype(BF16))
    for h, ((i, vh, c), eg, gcb) in enumerate(zip(heads, egs, gcbs)):
        rs = rows_of(c)
        sol = sols[h // PACK][(h % PACK) * CHUNK:(h % PACK + 1) * CHUNK]
        g_last = gcb[CHUNK - 1:CHUNK, :]
        u_ref[vh, rs, :] = sol[:, :HEAD_DIM]
        w_ref[vh, rs, :] = sol[:, HEAD_DIM:].astype(BF16)
        qd_ref[vh, rs, :] = (qs[i] * eg).astype(BF16)
        gt_ref[vh * n_chunks + c] = jnp.exp(g_last)
    for i, (n, c) in enumerate(pairs):
        g_last = gcb_pairs[i][CHUNK - 1:CHUNK, :]
        kdt_ref[n * n_chunks + c] = (k_ts[i] * jnp.exp(g_last - g_row_pairs[i])).astype(BF16)

    zero_s = jnp.zeros((HEAD_DIM, HEAD_DIM), BF16)
    zero_v = jnp.zeros((CHUNK, HEAD_DIM), BF16)
    diag2 = lambda a, b, zero: jnp.concatenate([jnp.concatenate([a, zero], axis=1),
                                                jnp.concatenate([zero, b], axis=1)], axis=0)
    all_pairs = range(bt * QK_HEADS)
    for c in range(n_chunks):
        rs = rows_of(c)
        ss = [s_ref[vh] for vh in range(bt * V_HEADS)]
        ws_qs = [_mm(jnp.concatenate([jnp.concatenate([w_ref[2 * n, rs, :], w_ref[2 * n + 1, rs, :]], axis=1),
                                      jnp.concatenate([qd_ref[2 * n, rs, :], qd_ref[2 * n + 1, rs, :]], axis=1)], axis=0),
                     diag2(ss[2 * n].astype(BF16), ss[2 * n + 1].astype(BF16), zero_s))
                 for n in all_pairs]
        v_news = [[(u_ref[2 * n + hh, rs, :] - ws_qs[n][:CHUNK, hh * HEAD_DIM:(hh + 1) * HEAD_DIM]).astype(BF16)
                   for hh in range(2)] for n in all_pairs]
        for g in range(bt * V_HEADS // PACK):
            stacked = [v_news[(g * PACK + j) // 2][(g * PACK + j) % 2] for j in range(PACK)]
            qk_v = _mm(qkd_ref[g * n_chunks + c], jnp.concatenate(stacked, axis=0))
            for j in range(PACK):
                vh = g * PACK + j
                o_ref[vh, rs, :] = (ws_qs[vh // 2][CHUNK:, (vh % 2) * HEAD_DIM:(vh % 2 + 1) * HEAD_DIM]
                                    + qk_v[j * CHUNK:(j + 1) * CHUNK])
        for n in all_pairs:
            upd = _mm(kdt_ref[n * n_chunks + c], diag2(v_news[n][0], v_news[n][1], zero_v))
            for hh in range(2):
                vh = 2 * n + hh
                s_ref[vh] = ss[vh] * gt_ref[vh * n_chunks + c] + upd[:, hh * HEAD_DIM:(hh + 1) * HEAD_DIM]

    z = z_ref[...]
    gated = []
    for hh in range(V_HEADS):
        o_h = jnp.concatenate([o_ref[b * V_HEADS + hh] for b in range(bt)], axis=0)
        gated.append((_rms(o_h, gnorm_ref[...]) * _silu(z[:, hh * HEAD_DIM:(hh + 1) * HEAD_DIM])).astype(BF16))
    m = _mm(jnp.concatenate(gated, axis=1), wout_ref[...])
    out_ref[...] = x + gate * _rms(m, gpost_ref[...]).reshape(bt, tt, d)


def _gdn_mixer(x, mod, hist, s0, p, *, bt, tt):
    bsz, seq, d = x.shape
    ch = hist.shape[-1]
    v_width = V_HEADS * HEAD_DIM
    assert bsz % bt == 0 and seq % tt == 0 and tt % CHUNK == 0 and HEAD_DIM == LANES
    tok = pl.BlockSpec((bt, tt, d), lambda b, t: (b, t, 0))
    per_batch = lambda rows, cols: pl.BlockSpec((bt, rows, cols), lambda b, t: (b, 0, 0))
    state = pl.BlockSpec((bt * V_HEADS, HEAD_DIM, HEAD_DIM), lambda b, t: (b, 0, 0))
    heads = lambda n: pltpu.VMEM((bt * n, tt, HEAD_DIM), F32)
    return pl.pallas_call(
        _gdn_kernel,
        grid=(bsz // bt, seq // tt),
        in_specs=[tok, per_batch(1, 3 * d), per_batch(SHORT_HIST, ch), state,
                  _const_spec((1, d)), _const_spec((d, ch)), _const_spec((d, v_width)), _const_spec((d, 2 * LANES)),
                  _const_spec((SHORT_WIDTH, ch)), _const_spec((1, LANES)), _const_spec((1, LANES)),
                  _const_spec((1, HEAD_DIM)), _const_spec((v_width, d)), _const_spec((1, d))],
        out_specs=[tok, per_batch(SHORT_HIST, ch), state],
        out_shape=[jax.ShapeDtypeStruct(x.shape, F32), jax.ShapeDtypeStruct((bsz, SHORT_HIST, ch), F32),
                   jax.ShapeDtypeStruct(s0.shape, F32)],
        scratch_shapes=[pltpu.VMEM((ch // LANES, bt, SHORT_PAD + tt, LANES), F32),
                        heads(QK_HEADS), heads(QK_HEADS), heads(V_HEADS), heads(V_HEADS),
                        heads(V_HEADS), heads(V_HEADS), pltpu.VMEM((bt * V_HEADS, 1, tt), F32),
                        heads(V_HEADS), pltpu.VMEM((bt * V_HEADS, tt, HEAD_DIM), BF16),
                        pltpu.VMEM((bt * V_HEADS, tt, HEAD_DIM), BF16),
                        pltpu.VMEM((bt * V_HEADS // PACK * (tt // CHUNK), PACK * CHUNK, PACK * CHUNK), BF16),
                        pltpu.VMEM((bt * QK_HEADS * (tt // CHUNK), HEAD_DIM, 2 * CHUNK), BF16),
                        pltpu.VMEM((bt * V_HEADS * (tt // CHUNK), 1, HEAD_DIM), F32),
                        pltpu.VMEM((bt * tt, v_width), F32)],
        compiler_params=_params(2),
        name="gdn_mixer",
    )(x, mod, hist, s0, p["g_pre"], p["w_qkv"], p["w_z"], p["w_ab"], p["w_conv"], p["a_log"], p["dt_bias"],
      p["g_norm"], p["w_out"], p["g_post"])


def _row(v):
    return v.reshape(1, -1)


def _pad_lanes(v):
    return jnp.pad(v.reshape(1, -1), ((0, 0), (0, LANES - v.shape[-1])))


def _trunk(x, mod, conv_hist, s0, gdn_hist, w, *, conv_tile, ffn_tile, gdn_tile):
    bsz = x.shape[0]
    x, conv_cache = _conv_mixer(x, mod[0], conv_hist, w["conv"], bt=conv_tile[0], tt=conv_tile[1])
    x = _ffn(x, mod[0], w["ffn"][0], bt=ffn_tile[0], tt=ffn_tile[1])
    x, gdn_cache, s = _gdn_mixer(x, mod[1], gdn_hist, s0.reshape(bsz * V_HEADS, HEAD_DIM, HEAD_DIM), w["gdn"],
                                 bt=gdn_tile[0], tt=gdn_tile[1])
    x = _ffn(x, mod[1], w["ffn"][1], bt=ffn_tile[0], tt=ffn_tile[1])
    return x, conv_cache[None], s.reshape((1,) + s0.shape), gdn_cache[None]


def kernel(x_prompt, x_sample, c_prompt, c_sample, cache_conv, state_gdn, cache_gdn_conv, w_ada, b_ada, g_pre_mix, g_post_mix, g_pre_ffn, g_post_ffn, w_ffn_gate, w_ffn_up, w_ffn_down, w_pw1, b_pw1, w_dw, b_dw, ln_conv_g, ln_conv_b, w_pw2, b_pw2, w_gdn_in, w_gdn_conv, gdn_A_log, gdn_dt_bias, g_gdn_out_norm, w_gdn_out):
    n_prompt = x_prompt.shape[0]
    d = x_prompt.shape[-1]
    qkv_width = 2 * QK_HEADS * HEAD_DIM + V_HEADS * HEAD_DIM
    v_width = V_HEADS * HEAD_DIM

    mod = _modulation(jnp.concatenate([c_prompt, c_sample], axis=0), w_ada, b_ada)[:, :, None, :]
    mod_prompt, mod_sample = mod[:, :n_prompt], mod[:, n_prompt:]

    w_in = w_gdn_in[0]
    w_a = w_in[:, qkv_width + v_width:qkv_width + v_width + V_HEADS]
    w_b = w_in[:, qkv_width + v_width + V_HEADS:]
    lane_pad = ((0, 0), (0, LANES - V_HEADS))
    w = {
        "conv": dict(g_pre=_row(g_pre_mix[0]), w_pw1=w_pw1[0].astype(BF16), b_pw1=_row(b_pw1[0]), w_dw=w_dw[0],
                     b_dw=_row(b_dw[0]), ln_g=_row(ln_conv_g[0]), ln_b=_row(ln_conv_b[0]),
                     w_pw2=w_pw2[0].astype(BF16), b_pw2=_row(b_pw2[0]), g_post=_row(g_post_mix[0])),
        "ffn": [dict(g_pre=_row(g_pre_ffn[i]), w_gate=w_ffn_gate[i].astype(BF16), w_up=w_ffn_up[i].astype(BF16),
                     w_down=w_ffn_down[i].astype(BF16), g_post=_row(g_post_ffn[i])) for i in range(2)],
        "gdn": dict(g_pre=_row(g_pre_mix[1]), w_qkv=w_in[:, :qkv_width].astype(BF16),
                    w_z=w_in[:, qkv_width:qkv_width + v_width].astype(BF16),
                    w_ab=jnp.concatenate([jnp.pad(w_a, lane_pad), jnp.pad(w_b, lane_pad)], axis=1).astype(BF16),
                    w_conv=w_gdn_conv[0], a_log=_pad_lanes(gdn_A_log[0]), dt_bias=_pad_lanes(gdn_dt_bias[0]),
                    g_norm=_row(g_gdn_out_norm[0]), w_out=w_gdn_out[0].astype(BF16), g_post=_row(g_post_mix[1])),
    }

    zeros = lambda shape: jnp.zeros(shape, F32)
    y_prompt, conv_p, s_p, gconv_p = _trunk(
        x_prompt, mod_prompt, zeros((n_prompt,) + cache_conv.shape[2:]), zeros((n_prompt,) + state_gdn.shape[2:]),
        zeros((n_prompt,) + cache_gdn_conv.shape[2:]), w,
        conv_tile=(1, 256), ffn_tile=(1, 512), gdn_tile=(1, 256))
    y_sample, conv_s, s_s, gconv_s = _trunk(
        x_sample, mod_sample, cache_conv[0], state_gdn[0], cache_gdn_conv[0], w,
        conv_tile=(4, 64), ffn_tile=(8, 64), gdn_tile=(2, 64))
    return (y_prompt, y_sample, conv_p, conv_s, s_p, s_s, gconv_p, gconv_s)
```

```python
import functools

import jax
import jax.numpy as jnp
from jax import lax
from jax.experimental import pallas as pl
from jax.experimental.pallas import tpu as pltpu

F32 = jnp.float32
BF16 = jnp.bfloat16
EPS = 1e-6

SUBLANES = 8
LANES = 128
V7X_VMEM_BYTES = 64 * 1024 * 1024
VMEM_LIMIT = V7X_VMEM_BYTES * 7 // 8

CONV_WIDTH = 31
CONV_HIST = CONV_WIDTH - 1
CONV_PAD = 32
SHORT_WIDTH = 4
SHORT_HIST = SHORT_WIDTH - 1
SHORT_PAD = SUBLANES
HEAD_DIM = 128
QK_HEADS = 8
V_HEADS = 16
CHUNK = 64
TAP_STRIDE = 4
FFN_SUBTILES = 2
PACK = 2


def _rms(x, g):
    return x * lax.rsqrt(jnp.mean(x * x, axis=-1, keepdims=True) + EPS) * g


def _silu(x):
    return x * jax.nn.sigmoid(x)


def _softplus(x):
    return jnp.maximum(x, 0.0) + jnp.log1p(jnp.exp(-jnp.abs(x)))


def _mm(a, b):
    return jnp.dot(a, b, preferred_element_type=F32)


def _mm_nt(a, b):
    return lax.dot_general(a, b, (((1,), (1,)), ((), ())), preferred_element_type=F32)


def _const_spec(shape):
    zeros = (0,) * len(shape)
    return pl.BlockSpec(shape, lambda *_: zeros, pipeline_mode=pl.Buffered(1))


def _params(n_axes):
    return pltpu.CompilerParams(dimension_semantics=("arbitrary",) * n_axes, vmem_limit_bytes=VMEM_LIMIT)


def _modulation_kernel(c_ref, w_ref, b_ref, o_ref):
    c = c_ref[...]
    o_ref[...] = _mm(_silu(c).astype(BF16), w_ref[...].astype(BF16)) + b_ref[...]


def _modulation(c, w_ada, b_ada):
    depth, d, n = w_ada.shape
    rows = c.shape[0]
    bn = n // 4
    return pl.pallas_call(
        _modulation_kernel,
        grid=(depth, n // bn),
        in_specs=[
            pl.BlockSpec((rows, d), lambda i, j: (0, 0)),
            pl.BlockSpec((None, d, bn), lambda i, j: (i, 0, j)),
            pl.BlockSpec((None, 1, bn), lambda i, j: (i, 0, j)),
        ],
        out_specs=pl.BlockSpec((None, rows, bn), lambda i, j: (i, 0, j)),
        out_shape=jax.ShapeDtypeStruct((depth, rows, n), F32),
        compiler_params=_params(2),
        name="modulation",
    )(c, w_ada, b_ada.reshape(depth, 1, n))


def _conv_mixer_kernel(x_ref, mod_ref, hist_ref, gpre_ref, wpw1_ref, bpw1_ref, wdw_ref, bdw_ref,
                       lng_ref, lnb_ref, wpw2_ref, bpw2_ref, gpost_ref,
                       out_ref, cache_ref, ext_ref, y_ref, wb_ref):
    bt, tt, d = x_ref.shape
    slabs = d // LANES
    lanes_of = lambda sl: slice(sl * LANES, (sl + 1) * LANES)
    hist_rows = slice(CONV_PAD - CONV_HIST, CONV_PAD)

    @pl.when((pl.program_id(0) == 0) & (pl.program_id(1) == 0))
    def _():
        for k in range(CONV_WIDTH):
            wb_ref[k] = jnp.broadcast_to(wdw_ref[k:k + 1, :], (SUBLANES, d))

    @pl.when(pl.program_id(1) == 0)
    def _():
        for sl in range(slabs):
            ext_ref[sl, :, hist_rows, :] = hist_ref[:, :, lanes_of(sl)]

    x = x_ref[...]
    mod = mod_ref[...]
    shift, scale, gate = mod[:, :, :d], mod[:, :, d:2 * d], mod[:, :, 2 * d:]
    h = _rms(x, gpre_ref[...]) * (1.0 + scale) + shift
    a = _mm(h.reshape(bt * tt, d).astype(BF16), wpw1_ref[...]) + bpw1_ref[...]
    u = (a[:, :d] * jax.nn.sigmoid(a[:, d:])).reshape(bt, tt, d)
    for sl in range(slabs):
        ext_ref[sl, :, CONV_PAD:CONV_PAD + tt, :] = u[:, :, lanes_of(sl)]

    group = SUBLANES * TAP_STRIDE
    for sl in range(slabs):
        w_taps = [wb_ref[k, :, lanes_of(sl)] for k in range(CONV_WIDTH)]
        bias = jnp.broadcast_to(bdw_ref[:, lanes_of(sl)], (SUBLANES, LANES))
        for g0 in range(0, tt, group):
            for b in range(bt):
                accs = [bias] * TAP_STRIDE
                for k in range(CONV_WIDTH):
                    for r in range(TAP_STRIDE):
                        rows = pl.ds(g0 + (CONV_PAD - CONV_HIST + k + r), SUBLANES, stride=TAP_STRIDE)
                        accs[r] = accs[r] + ext_ref[sl, b, rows, :] * w_taps[k]
                for r in range(TAP_STRIDE):
                    y_ref[sl, b, pl.ds(g0 + r, SUBLANES, stride=TAP_STRIDE), :] = accs[r]

    y = jnp.concatenate([y_ref[sl] for sl in range(slabs)], axis=-1).reshape(bt * tt, d)
    mu = jnp.mean(y, axis=-1, keepdims=True)
    yc = y - mu
    var = jnp.mean(yc * yc, axis=-1, keepdims=True)
    yn = yc * lax.rsqrt(var + EPS) * lng_ref[...] + lnb_ref[...]
    m = _mm(_silu(yn).astype(BF16), wpw2_ref[...]) + bpw2_ref[...]
    out_ref[...] = x + gate * _rms(m, gpost_ref[...]).reshape(bt, tt, d)

    for sl in range(slabs):
        tail = ext_ref[sl, :, CONV_PAD + tt - CONV_HIST:CONV_PAD + tt, :]
        cache_ref[:, :, lanes_of(sl)] = tail
        ext_ref[sl, :, hist_rows, :] = tail


def _conv_mixer(x, mod, hist, p, *, bt, tt):
    bsz, seq, d = x.shape
    assert bsz % bt == 0 and seq % tt == 0 and tt % (SUBLANES * TAP_STRIDE) == 0 and tt >= CONV_HIST
    tok = pl.BlockSpec((bt, tt, d), lambda b, t: (b, t, 0))
    per_batch = lambda rows, cols: pl.BlockSpec((bt, rows, cols), lambda b, t: (b, 0, 0))
    return pl.pallas_call(
        _conv_mixer_kernel,
        grid=(bsz // bt, seq // tt),
        in_specs=[tok, per_batch(1, 3 * d), per_batch(CONV_HIST, d),
                  _const_spec((1, d)), _const_spec((d, 2 * d)), _const_spec((1, 2 * d)),
                  _const_spec((CONV_WIDTH, d)), _const_spec((1, d)), _const_spec((1, d)), _const_spec((1, d)),
                  _const_spec((d, d)), _const_spec((1, d)), _const_spec((1, d))],
        out_specs=[tok, per_batch(CONV_HIST, d)],
        out_shape=[jax.ShapeDtypeStruct(x.shape, F32), jax.ShapeDtypeStruct((bsz, CONV_HIST, d), F32)],
        scratch_shapes=[pltpu.VMEM((d // LANES, bt, CONV_PAD + tt, LANES), F32),
                        pltpu.VMEM((d // LANES, bt, tt, LANES), F32), pltpu.VMEM((CONV_WIDTH, SUBLANES, d), F32)],
        compiler_params=_params(2),
        name="conv_mixer",
    )(x, mod, hist, p["g_pre"], p["w_pw1"], p["b_pw1"], p["w_dw"], p["b_dw"], p["ln_g"], p["ln_b"],
      p["w_pw2"], p["b_pw2"], p["g_post"])


def _ffn_kernel(x_ref, mod_ref, gpre_ref, wg_ref, wu_ref, wd_ref, gpost_ref, out_ref):
    bt, tt, d = x_ref.shape
    mod = mod_ref[...]
    if bt >= FFN_SUBTILES:
        parts = [(slice(i * bt // FFN_SUBTILES, (i + 1) * bt // FFN_SUBTILES), slice(None)) for i in range(FFN_SUBTILES)]
    else:
        parts = [(slice(None), slice(i * tt // FFN_SUBTILES, (i + 1) * tt // FFN_SUBTILES)) for i in range(FFN_SUBTILES)]
    for bs, ts in parts:
        x = x_ref[bs, ts, :]
        pb, pt, _ = x.shape
        m = mod[bs]
        shift, scale, gate = m[:, :, :d], m[:, :, d:2 * d], m[:, :, 2 * d:]
        h = (_rms(x, gpre_ref[...]) * (1.0 + scale) + shift).reshape(pb * pt, d).astype(BF16)
        hid = _silu(_mm(h, wg_ref[...])) * _mm(h, wu_ref[...])
        f = _mm(hid.astype(BF16), wd_ref[...])
        out_ref[bs, ts, :] = x + gate * _rms(f, gpost_ref[...]).reshape(pb, pt, d)


def _ffn(x, mod, p, *, bt, tt):
    bsz, seq, d = x.shape
    f = p["w_gate"].shape[1]
    assert bsz % bt == 0 and seq % tt == 0
    tok = pl.BlockSpec((bt, tt, d), lambda b, t: (b, t, 0))
    return pl.pallas_call(
        _ffn_kernel,
        grid=(bsz // bt, seq // tt),
        in_specs=[tok, pl.BlockSpec((bt, 1, 3 * d), lambda b, t: (b, 0, 1)),
                  _const_spec((1, d)), _const_spec((d, f)), _const_spec((d, f)), _const_spec((f, d)),
                  _const_spec((1, d))],
        out_specs=tok,
        out_shape=jax.ShapeDtypeStruct(x.shape, F32),
        compiler_params=_params(2),
        name="ffn",
    )(x, mod, p["g_pre"], p["w_gate"], p["w_up"], p["w_down"], p["g_post"])


def _packed_unit_lower_inverses(ms):
    n, width = ms[0].shape
    r = lax.broadcasted_iota(jnp.int32, (n, width), 0)
    c = lax.broadcasted_iota(jnp.int32, (n, width), 1) & (n - 1)
    big_r = lax.broadcasted_iota(jnp.int32, (width, width), 0)
    big_c = lax.broadcasted_iota(jnp.int32, (width, width), 1)
    same_block = (big_r // n) == (big_c // n)
    rr, cc = big_r & (n - 1), big_c & (n - 1)
    stack = lambda y: jnp.concatenate([y] * (width // n), axis=0)
    bd_mask = same_block.astype(BF16)
    mbs = [stack(m.astype(BF16)) for m in ms]
    xs = [(r == c).astype(F32) - jnp.where((r >> 1) == (c >> 1), m, 0.0) for m in ms]
    level = 1
    while (1 << level) < n:
        couple = same_block & ((rr >> (level + 1)) == (cc >> (level + 1))) & ((rr >> level) != (cc >> level))
        couple = couple.astype(BF16)
        xbs = [x.astype(BF16) for x in xs]
        xcs = [_mm(xb, mb * couple).astype(BF16) for xb, mb in zip(xbs, mbs)]
        xs = [x - _mm(xc, stack(xb) * bd_mask) for x, xc, xb in zip(xs, xcs, xbs)]
        level += 1
    return xs


def _gdn_kernel(x_ref, mod_ref, hist_ref, s0_ref, gpre_ref, wqkv_ref, wz_ref, wab_ref, wconv_ref,
                alog_ref, dtb_ref, gnorm_ref, wout_ref, gpost_ref, tri_ref,
                out_ref, cache_ref, s_ref,
                ext_ref, q_ref, k_ref, v_ref, o_ref, gcb_ref, bb_ref, gr_ref,
                u_ref, w_ref, qd_ref, qkd_ref, kdt_ref, gt_ref, z_ref):
    bt, tt, d = x_ref.shape
    rows = bt * tt
    n_chunks = tt // CHUNK
    slabs = wqkv_ref.shape[1] // LANES
    lanes_of = lambda sl: slice(sl * LANES, (sl + 1) * LANES)
    hist_rows = slice(SHORT_PAD - SHORT_HIST, SHORT_PAD)

    @pl.when(pl.program_id(1) == 0)
    def _():
        for sl in range(slabs):
            ext_ref[sl, :, hist_rows, :] = hist_ref[:, :, lanes_of(sl)]
        s_ref[...] = s0_ref[...]

    x = x_ref[...]
    mod = mod_ref[...]
    shift, scale, gate = mod[:, :, :d], mod[:, :, d:2 * d], mod[:, :, 2 * d:]
    hb = (_rms(x, gpre_ref[...]) * (1.0 + scale) + shift).reshape(rows, d).astype(BF16)

    ab = _mm(hb, wab_ref[...])
    g = -jnp.exp(alog_ref[...]) * _softplus(ab[:, :LANES] + dtb_ref[...])
    beta = jax.nn.sigmoid(ab[:, LANES:])
    g_hi = g.astype(BF16)
    g_rest = g - g_hi.astype(F32)
    g_mid = g_rest.astype(BF16)
    g_lo = (g_rest - g_mid.astype(F32)).astype(BF16)
    tri = tri_ref[...]
    gc = _mm(tri, g_hi) + _mm(tri, g_mid) + _mm(tri, g_lo)
    pad = (-rows) % LANES
    gc_t = (jnp.concatenate([gc, jnp.zeros((pad, LANES), F32)], axis=0) if pad else gc).T
    for b in range(bt):
        for hh in range(V_HEADS):
            n = b * V_HEADS + hh
            gcb_ref[n] = jnp.broadcast_to(gc[b * tt:(b + 1) * tt, hh:hh + 1], (tt, LANES))
            bb_ref[n] = jnp.broadcast_to(beta[b * tt:(b + 1) * tt, hh:hh + 1], (tt, LANES))
            gr_ref[n] = gc_t[hh:hh + 1, b * tt:(b + 1) * tt]

    pre = _mm(hb, wqkv_ref[...]).reshape(bt, tt, slabs * LANES)
    z_ref[...] = _mm(hb, wz_ref[...])
    for sl in range(slabs):
        ext_ref[sl, :, SHORT_PAD:SHORT_PAD + tt, :] = pre[:, :, lanes_of(sl)]
    group = SUBLANES * TAP_STRIDE
    for sl in range(slabs):
        w_taps = [jnp.broadcast_to(wconv_ref[k:k + 1, lanes_of(sl)], (SUBLANES, LANES)) for k in range(SHORT_WIDTH)]
        for b in range(bt):
            for t0 in range(0, tt, group):
                for r in range(TAP_STRIDE):
                    acc = None
                    for k in range(SHORT_WIDTH):
                        taps = pl.ds(t0 + r + (SHORT_PAD - SHORT_HIST + k), SUBLANES, stride=TAP_STRIDE)
                        term = ext_ref[sl, b, taps, :] * w_taps[k]
                        acc = term if acc is None else acc + term
                    val = _silu(acc)
                    out_rows = pl.ds(t0 + r, SUBLANES, stride=TAP_STRIDE)
                    if sl < 2 * QK_HEADS:
                        inv_norm = lax.rsqrt(jnp.sum(val * val, axis=-1, keepdims=True) + EPS)
                    if sl < QK_HEADS:
                        q_ref[b * QK_HEADS + sl, out_rows, :] = val * (inv_norm * (HEAD_DIM ** -0.5))
                    elif sl < 2 * QK_HEADS:
                        k_ref[b * QK_HEADS + sl - QK_HEADS, out_rows, :] = val * inv_norm
                    else:
                        v_ref[b * V_HEADS + sl - 2 * QK_HEADS, out_rows, :] = val
    for sl in range(slabs):
        tail = ext_ref[sl, :, SHORT_PAD + tt - SHORT_HIST:SHORT_PAD + tt, :]
        cache_ref[:, :, lanes_of(sl)] = tail
        ext_ref[sl, :, hist_rows, :] = tail

    width = PACK * CHUNK
    r_c = lax.broadcasted_iota(jnp.int32, (CHUNK, width), 0)
    c_c = lax.broadcasted_iota(jnp.int32, (CHUNK, width), 1) & (CHUNK - 1)
    incl = r_c >= c_c
    strict = r_c > c_c
    low_half = lax.broadcasted_iota(jnp.int32, (CHUNK, HEAD_DIM), 1) < CHUNK
    bd_mask = ((lax.broadcasted_iota(jnp.int32, (width, width), 0) // CHUNK)
               == (lax.broadcasted_iota(jnp.int32, (width, width), 1) // CHUNK)).astype(BF16)
    block_diag = lambda y: jnp.concatenate([y] * PACK, axis=0) * bd_mask
    rows_of = lambda c: slice(c * CHUNK, (c + 1) * CHUNK)

    packs = [(g, c) for g in range(bt * V_HEADS // PACK) for c in range(n_chunks)]
    pairs = [(g * (PACK // 2) + j, c) for g, c in packs for j in range(PACK // 2)]
    heads = [(i, 2 * n + hh, c) for i, (n, c) in enumerate(pairs) for hh in range(2)]
    qs = [q_ref[n, rows_of(c), :] for n, c in pairs]
    ks = [k_ref[n, rows_of(c), :] for n, c in pairs]
    kbs = [k.astype(BF16) for k in ks]
    prods = [_mm_nt(jnp.concatenate([q.astype(BF16), kb], axis=0), jnp.concatenate([kb, kb], axis=0))
             for q, kb in zip(qs, kbs)]
    k_ts = [jnp.concatenate([k, k], axis=0).T for k in ks]
    gcbs = [gcb_ref[vh, rows_of(c), :] for _, vh, c in heads]
    bbs = [bb_ref[vh, rows_of(c), :] for _, vh, c in heads]
    g_rows = [gr_ref[vh][:, rows_of(c)] for _, vh, c in heads]
    side_by_side = lambda per_pair, p: jnp.concatenate(per_pair[p * (PACK // 2):(p + 1) * (PACK // 2)], axis=1)
    pair_lanes = lambda vals: [jnp.where(low_half, vals[2 * i], vals[2 * i + 1]) for i in range(len(pairs))]
    gcb_pairs, bb_pairs = pair_lanes(gcbs), pair_lanes(bbs)
    g_row_pairs = [jnp.concatenate([g_rows[2 * i], g_rows[2 * i + 1]], axis=1) for i in range(len(pairs))]
    decays, m_packs = [], []
    for p in range(len(packs)):
        diff = side_by_side(gcb_pairs, p) - side_by_side(g_row_pairs, p)
        decay = jnp.where(incl, jnp.exp(jnp.where(incl, diff, 0.0)), 0.0)
        a_kk = side_by_side([pr[CHUNK:] for pr in prods], p)
        decays.append(decay)
        m_packs.append(jnp.where(strict, a_kk * side_by_side(bb_pairs, p) * decay, 0.0))
    t_invs = _packed_unit_lower_inverses(m_packs)
    egs = [jnp.exp(gcb) for gcb in gcbs]
    rhss = [jnp.concatenate([v_ref[vh, rows_of(c), :] * bb, ks[i] * (bb * eg)], axis=1).astype(BF16)
            for (i, vh, c), bb, eg in zip(heads, bbs, egs)]
    sols = [_mm(block_diag(t_inv.astype(BF16)), jnp.concatenate(rhss[p * PACK:(p + 1) * PACK], axis=0))
            for p, t_inv in enumerate(t_invs)]
    for p, (g, c) in enumerate(packs):
        a_qk = side_by_side([pr[:CHUNK] for pr in prods], p)
        qkd_ref[g * n_chunks + c] = block_diag((a_qk * decays[p]).astype(BF16))
    for h, ((i, vh, c), eg, gcb) in enumerate(zip(heads, egs, gcbs)):
        rs = rows_of(c)
        sol = sols[h // PACK][(h % PACK) * CHUNK:(h % PACK + 1) * CHUNK]
        g_last = gcb[CHUNK - 1:CHUNK, :]
        u_ref[vh, rs, :] = sol[:, :HEAD_DIM]
        w_ref[vh, rs, :] = sol[:, HEAD_DIM:].astype(BF16)
        qd_ref[vh, rs, :] = (qs[i] * eg).astype(BF16)
        gt_ref[vh * n_chunks + c] = jnp.exp(g_last)
    for i, (n, c) in enumerate(pairs):
        g_last = gcb_pairs[i][CHUNK - 1:CHUNK, :]
        kdt_ref[n * n_chunks + c] = (k_ts[i] * jnp.exp(g_last - g_row_pairs[i])).astype(BF16)

    zero_s = jnp.zeros((HEAD_DIM, HEAD_DIM), BF16)
    zero_v = jnp.zeros((CHUNK, HEAD_DIM), BF16)
    diag2 = lambda a, b, zero: jnp.concatenate([jnp.concatenate([a, zero], axis=1),
                                                jnp.concatenate([zero, b], axis=1)], axis=0)
    all_pairs = range(bt * QK_HEADS)
    for c in range(n_chunks):
        rs = rows_of(c)
        ss = [s_ref[vh] for vh in range(bt * V_HEADS)]
        ws_qs = [_mm(jnp.concatenate([jnp.concatenate([w_ref[2 * n, rs, :], w_ref[2 * n + 1, rs, :]], axis=1),
                                      jnp.concatenate([qd_ref[2 * n, rs, :], qd_ref[2 * n + 1, rs, :]], axis=1)], axis=0),
                     diag2(ss[2 * n].astype(BF16), ss[2 * n + 1].astype(BF16), zero_s))
                 for n in all_pairs]
        v_news = [[(u_ref[2 * n + hh, rs, :] - ws_qs[n][:CHUNK, hh * HEAD_DIM:(hh + 1) * HEAD_DIM]).astype(BF16)
                   for hh in range(2)] for n in all_pairs]
        for g in range(bt * V_HEADS // PACK):
            stacked = [v_news[(g * PACK + j) // 2][(g * PACK + j) % 2] for j in range(PACK)]
            qk_v = _mm(qkd_ref[g * n_chunks + c], jnp.concatenate(stacked, axis=0))
            for j in range(PACK):
                vh = g * PACK + j
                o_ref[vh, rs, :] = (ws_qs[vh // 2][CHUNK:, (vh % 2) * HEAD_DIM:(vh % 2 + 1) * HEAD_DIM]
                                    + qk_v[j * CHUNK:(j + 1) * CHUNK])
        for n in all_pairs:
            upd = _mm(kdt_ref[n * n_chunks + c], diag2(v_news[n][0], v_news[n][1], zero_v))
            for hh in range(2):
                vh = 2 * n + hh
                s_ref[vh] = ss[vh] * gt_ref[vh * n_chunks + c] + upd[:, hh * HEAD_DIM:(hh + 1) * HEAD_DIM]

    z = z_ref[...]
    gated = []
    for hh in range(V_HEADS):
        o_h = jnp.concatenate([o_ref[b * V_HEADS + hh] for b in range(bt)], axis=0)
        gated.append((_rms(o_h, gnorm_ref[...]) * _silu(z[:, hh * HEAD_DIM:(hh + 1) * HEAD_DIM])).astype(BF16))
    m = _mm(jnp.concatenate(gated, axis=1), wout_ref[...])
    out_ref[...] = x + gate * _rms(m, gpost_ref[...]).reshape(bt, tt, d)


def _chunk_cumsum_matrix(rows):
    r = lax.broadcasted_iota(jnp.int32, (rows, rows), 0)
    c = lax.broadcasted_iota(jnp.int32, (rows, rows), 1)
    return ((r >= c) & (r // CHUNK == c // CHUNK)).astype(BF16)


def _gdn_mixer(x, mod, hist, s0, p, *, bt, tt):
    bsz, seq, d = x.shape
    ch = hist.shape[-1]
    v_width = V_HEADS * HEAD_DIM
    assert bsz % bt == 0 and seq % tt == 0 and tt % CHUNK == 0 and HEAD_DIM == LANES
    tok = pl.BlockSpec((bt, tt, d), lambda b, t: (b, t, 0))
    per_batch = lambda rows, cols: pl.BlockSpec((bt, rows, cols), lambda b, t: (b, 0, 0))
    state = pl.BlockSpec((bt * V_HEADS, HEAD_DIM, HEAD_DIM), lambda b, t: (b, 0, 0))
    heads = lambda n: pltpu.VMEM((bt * n, tt, HEAD_DIM), F32)
    return pl.pallas_call(
        _gdn_kernel,
        grid=(bsz // bt, seq // tt),
        in_specs=[tok, per_batch(1, 3 * d), per_batch(SHORT_HIST, ch), state,
                  _const_spec((1, d)), _const_spec((d, ch)), _const_spec((d, v_width)), _const_spec((d, 2 * LANES)),
                  _const_spec((SHORT_WIDTH, ch)), _const_spec((1, LANES)), _const_spec((1, LANES)),
                  _const_spec((1, HEAD_DIM)), _const_spec((v_width, d)), _const_spec((1, d)),
                  _const_spec((bt * tt, bt * tt))],
        out_specs=[tok, per_batch(SHORT_HIST, ch), state],
        out_shape=[jax.ShapeDtypeStruct(x.shape, F32), jax.ShapeDtypeStruct((bsz, SHORT_HIST, ch), F32),
                   jax.ShapeDtypeStruct(s0.shape, F32)],
        scratch_shapes=[pltpu.VMEM((ch // LANES, bt, SHORT_PAD + tt, LANES), F32),
                        heads(QK_HEADS), heads(QK_HEADS), heads(V_HEADS), heads(V_HEADS),
                        heads(V_HEADS), heads(V_HEADS), pltpu.VMEM((bt * V_HEADS, 1, tt), F32),
                        heads(V_HEADS), pltpu.VMEM((bt * V_HEADS, tt, HEAD_DIM), BF16),
                        pltpu.VMEM((bt * V_HEADS, tt, HEAD_DIM), BF16),
                        pltpu.VMEM((bt * V_HEADS // PACK * (tt // CHUNK), PACK * CHUNK, PACK * CHUNK), BF16),
                        pltpu.VMEM((bt * QK_HEADS * (tt // CHUNK), HEAD_DIM, 2 * CHUNK), BF16),
                        pltpu.VMEM((bt * V_HEADS * (tt // CHUNK), 1, HEAD_DIM), F32),
                        pltpu.VMEM((bt * tt, v_width), F32)],
        compiler_params=_params(2),
        name="gdn_mixer",
    )(x, mod, hist, s0, p["g_pre"], p["w_qkv"], p["w_z"], p["w_ab"], p["w_conv"], p["a_log"], p["dt_bias"],
      p["g_norm"], p["w_out"], p["g_post"], _chunk_cumsum_matrix(bt * tt))


def _row(v):
    return v.reshape(1, -1)


def _pad_lanes(v):
    return jnp.pad(v.reshape(1, -1), ((0, 0), (0, LANES - v.shape[-1])))


def _trunk(x, mod, conv_hist, s0, gdn_hist, w, *, conv_tile, ffn_tile, gdn_tile):
    bsz = x.shape[0]
    x, conv_cache = _conv_mixer(x, mod[0], conv_hist, w["conv"], bt=conv_tile[0], tt=conv_tile[1])
    x = _ffn(x, mod[0], w["ffn"][0], bt=ffn_tile[0], tt=ffn_tile[1])
    x, gdn_cache, s = _gdn_mixer(x, mod[1], gdn_hist, s0.reshape(bsz * V_HEADS, HEAD_DIM, HEAD_DIM), w["gdn"],
                                 bt=gdn_tile[0], tt=gdn_tile[1])
    x = _ffn(x, mod[1], w["ffn"][1], bt=ffn_tile[0], tt=ffn_tile[1])
    return x, conv_cache[None], s.reshape((1,) + s0.shape), gdn_cache[None]


def kernel(x_prompt, x_sample, c_prompt, c_sample, cache_conv, state_gdn, cache_gdn_conv, w_ada, b_ada, g_pre_mix, g_post_mix, g_pre_ffn, g_post_ffn, w_ffn_gate, w_ffn_up, w_ffn_down, w_pw1, b_pw1, w_dw, b_dw, ln_conv_g, ln_conv_b, w_pw2, b_pw2, w_gdn_in, w_gdn_conv, gdn_A_log, gdn_dt_bias, g_gdn_out_norm, w_gdn_out):
    n_prompt = x_prompt.shape[0]
    d = x_prompt.shape[-1]
    qkv_width = 2 * QK_HEADS * HEAD_DIM + V_HEADS * HEAD_DIM
    v_width = V_HEADS * HEAD_DIM

    mod = _modulation(jnp.concatenate([c_prompt, c_sample], axis=0), w_ada, b_ada)[:, :, None, :]
    mod_prompt, mod_sample = mod[:, :n_prompt], mod[:, n_prompt:]

    w_in = w_gdn_in[0]
    w_a = w_in[:, qkv_width + v_width:qkv_width + v_width + V_HEADS]
    w_b = w_in[:, qkv_width + v_width + V_HEADS:]
    lane_pad = ((0, 0), (0, LANES - V_HEADS))
    w = {
        "conv": dict(g_pre=_row(g_pre_mix[0]), w_pw1=w_pw1[0].astype(BF16), b_pw1=_row(b_pw1[0]), w_dw=w_dw[0],
                     b_dw=_row(b_dw[0]), ln_g=_row(ln_conv_g[0]), ln_b=_row(ln_conv_b[0]),
                     w_pw2=w_pw2[0].astype(BF16), b_pw2=_row(b_pw2[0]), g_post=_row(g_post_mix[0])),
        "ffn": [dict(g_pre=_row(g_pre_ffn[i]), w_gate=w_ffn_gate[i].astype(BF16), w_up=w_ffn_up[i].astype(BF16),
                     w_down=w_ffn_down[i].astype(BF16), g_post=_row(g_post_ffn[i])) for i in range(2)],
        "gdn": dict(g_pre=_row(g_pre_mix[1]), w_qkv=w_in[:, :qkv_width].astype(BF16),
                    w_z=w_in[:, qkv_width:qkv_width + v_width].astype(BF16),
                    w_ab=jnp.concatenate([jnp.pad(w_a, lane_pad), jnp.pad(w_b, lane_pad)], axis=1).astype(BF16),
                    w_conv=w_gdn_conv[0], a_log=_pad_lanes(gdn_A_log[0]), dt_bias=_pad_lanes(gdn_dt_bias[0]),
                    g_norm=_row(g_gdn_out_norm[0]), w_out=w_gdn_out[0].astype(BF16), g_post=_row(g_post_mix[1])),
    }

    zeros = lambda shape: jnp.zeros(shape, F32)
    y_prompt, conv_p, s_p, gconv_p = _trunk(
        x_prompt, mod_prompt, zeros((n_prompt,) + cache_conv.shape[2:]), zeros((n_prompt,) + state_gdn.shape[2:]),
        zeros((n_prompt,) + cache_gdn_conv.shape[2:]), w,
        conv_tile=(1, 1024), ffn_tile=(1, 1024), gdn_tile=(1, 256))
    y_sample, conv_s, s_s, gconv_s = _trunk(
        x_sample, mod_sample, cache_conv[0], state_gdn[0], cache_gdn_conv[0], w,
        conv_tile=(4, 64), ffn_tile=(8, 64), gdn_tile=(2, 64))
    return (y_prompt, y_sample, conv_p, conv_s, s_p, s_s, gconv_p, gconv_s)
```
